```python
import math
import jax, jax.numpy as jnp
from jax import lax
import numpy as np

D_MODEL = 1024
BATCH = 8
SEQ = 4096
DEPTH = 2
DEC_BATCH = 32
DEC_SEQ = 16
PAST_LEN = 4096

CHUNK = 64
H_A = 8
KV_A = 2
DH_A = 64
GROUP_A = H_A // KV_A
H_I = 8
D_IDX = 64
TOPK_MAX = 256
Q_BLOCK = 64
INDEX_SCALE = (H_I * D_IDX) ** -0.5
H_B = 4
DK_B = 128
DV_B = 128
CONV_W = 4
CONV_DIM = 2 * H_B * DK_B + H_B * DV_B
D_FF = int(math.ceil(8 * D_MODEL / 3 / 256)) * 256
DEEPNORM_ALPHA = (2 * DEPTH) ** 0.25
DEEPNORM_BETA = (8 * DEPTH) ** -0.25
LN_EPS = 1e-5
NORM_EPS = 1e-6
IN_SIZES = (H_A * DH_A, KV_A * DH_A, KV_A * DH_A, H_I * D_IDX, D_IDX, H_I, CONV_DIM, H_B, H_B, H_B * DV_B, 2 * D_MODEL)
D_IN = sum(IN_SIZES)

kernel_name = "dsa_gated_deltanet_hybrid_stream_step"


def layer_norm(x, g, b):
    xf = x.astype(jnp.float32)
    mu = jnp.mean(xf, axis=-1, keepdims=True)
    var = jnp.mean(jnp.square(xf - mu), axis=-1, keepdims=True)
    y = (xf - mu) * lax.rsqrt(var + LN_EPS) * g.astype(jnp.float32) + b.astype(jnp.float32)
    return y.astype(x.dtype)


def l2_normalize(x):
    xf = x.astype(jnp.float32)
    return xf * lax.rsqrt(jnp.sum(xf * xf, axis=-1, keepdims=True) + NORM_EPS)


def dsa_attention(q, q_idx, w_idx, k_all, v_all, kidx_all, past_len):
    B, T = q.shape[0], q.shape[1]
    L = k_all.shape[1]
    topk = min(TOPK_MAX, L // 4)
    qb = min(Q_BLOCK, T)
    nb = T // qb
    kidx_f = kidx_all.astype(jnp.float32)
    k_chunk = jnp.arange(L, dtype=jnp.int32) // CHUNK
    q_pos = past_len + jnp.arange(T, dtype=jnp.int32)

    def to_blocks(a):
        return jnp.moveaxis(a.reshape((B, nb, qb) + a.shape[2:]), 1, 0)

    def attend_block(args):
        qs, qis, wis, qpos = args
        q_chunk = qpos // CHUNK
        admissible = k_chunk[None, :] <= q_chunk[:, None]
        rel = jax.nn.relu(jnp.einsum("bthd,bsd->bths", qis.astype(jnp.float32), kidx_f))
        score = jnp.einsum("bth,bths->bts", wis.astype(jnp.float32) * INDEX_SCALE, rel)
        score = jnp.where(admissible[None], score, -jnp.inf)
        _, idx = lax.top_k(score, topk)
        valid = (idx // CHUNK) <= q_chunk[None, :, None]
        k_sel = jax.vmap(lambda kk, ii: kk[ii])(k_all, idx)
        v_sel = jax.vmap(lambda vv, ii: vv[ii])(v_all, idx)
        qg = qs.reshape(B, qb, KV_A, GROUP_A, DH_A)
        s = jnp.einsum("btgrd,btjgd->btgrj", qg, k_sel).astype(jnp.float32) * (DH_A ** -0.5)
        s = jnp.where(valid[:, :, None, None, :], s, -jnp.inf)
        p = jax.nn.softmax(s, axis=-1).astype(v_sel.dtype)
        o = jnp.einsum("btgrj,btjgd->btgrd", p, v_sel)
        return o.reshape(B, qb, H_A * DH_A)

    out = lax.map(attend_block, (to_blocks(q), to_blocks(q_idx), to_blocks(w_idx), q_pos.reshape(nb, qb)))
    return jnp.moveaxis(out, 0, 1).reshape(B, T, H_A * DH_A)


def gated_delta_rule(q, k, v, g, beta, s0):
    B, T, H = g.shape
    C = min(CHUNK, T)
    N = T // C

    def blocks(a):
        return jnp.moveaxis(a.reshape(B, N, C, H, a.shape[-1]), 3, 1).astype(jnp.float32)

    q, k, v = blocks(q), blocks(k), blocks(v)
    g = blocks(g[..., None])[..., 0]
    beta = blocks(beta[..., None])[..., 0]
    G = jnp.cumsum(g, axis=-1)
    causal = jnp.tril(jnp.ones((C, C), dtype=bool))
    strict = jnp.tril(jnp.ones((C, C), dtype=bool), -1)
    diff = G[..., :, None] - G[..., None, :]
    decay = jnp.where(causal, jnp.exp(jnp.where(causal, diff, 0.0)), 0.0)
    a_mat = jnp.where(strict, beta[..., None] * jnp.einsum("bhncd,bhnsd->bhncs", k, k) * decay, 0.0)
    rhs = jnp.concatenate([v * beta[..., None], k * (beta * jnp.exp(G))[..., None]], axis=-1)
    sol = lax.linalg.triangular_solve(a_mat, rhs, left_side=True, lower=True, unit_diagonal=True)
    u, w = sol[..., :DV_B], sol[..., DV_B:]
    qk = jnp.einsum("bhncd,bhnsd->bhncs", q, k) * decay
    q_dec = q * jnp.exp(G)[..., None]
    k_dec = k * jnp.exp(G[..., -1:] - G)[..., None]
    g_tot = jnp.exp(G[..., -1])

    def step(S, inp):
        qd, kd, uc, wc, qkc, gt = inp
        v_new = uc - jnp.einsum("bhcd,bhde->bhce", wc, S)
        o = jnp.einsum("bhcd,bhde->bhce", qd, S) + jnp.einsum("bhcs,bhse->bhce", qkc, v_new)
        S = S * gt[..., None, None] + jnp.einsum("bhcd,bhce->bhde", kd, v_new)
        return S, o

    xs = tuple(jnp.moveaxis(a, 2, 0) for a in (q_dec, k_dec, u, w, qk, g_tot))
    s_fin, o = lax.scan(step, s0.astype(jnp.float32), xs)
    o = jnp.transpose(o, (1, 0, 3, 2, 4)).reshape(B, T, H, DV_B)
    return o, s_fin.astype(s0.dtype)


def trunk_layer(x, past_k, past_v, past_kidx, conv_state, ssm_state,
                w_in, b_gate, conv_w, a_log, dt_bias, gdn_norm_w, w_proj_a, w_proj_b, w_out,
                ln1_g, ln1_b, w_up, w_down, ln2_g, ln2_b):
    B, T, _ = x.shape
    past_len = past_k.shape[1]
    f32 = jnp.float32
    h = jnp.einsum("btd,de->bte", x, w_in)
    q_a, k_a, v_a, q_i, k_i, w_i, qkv_b, a_b, b_b, gate_b, gates = jnp.split(
        h, np.cumsum(IN_SIZES)[:-1].tolist(), axis=-1)

    k_a = k_a.reshape(B, T, KV_A, DH_A)
    v_a = v_a.reshape(B, T, KV_A, DH_A)
    k_all = jnp.concatenate([past_k.astype(x.dtype), k_a], axis=1)
    v_all = jnp.concatenate([past_v.astype(x.dtype), v_a], axis=1)
    kidx_all = jnp.concatenate([past_kidx.astype(x.dtype), k_i], axis=1)
    y_a = dsa_attention(q_a.reshape(B, T, H_A, DH_A), q_i.reshape(B, T, H_I, D_IDX), w_i,
                        k_all, v_all, kidx_all, past_len)

    conv_in = jnp.concatenate([conv_state.astype(x.dtype), qkv_b], axis=1)
    conv_out = conv_in[:, 0:T] * conv_w[0]
    for j in range(1, CONV_W):
        conv_out = conv_out + conv_in[:, j:j + T] * conv_w[j]
    new_conv = conv_in[:, T:]
    qkv = jax.nn.silu(conv_out)
    q_b, k_b, v_b = jnp.split(qkv, [H_B * DK_B, 2 * H_B * DK_B], axis=-1)
    q_b = l2_normalize(q_b.reshape(B, T, H_B, DK_B)) * (DK_B ** -0.5)
    k_b = l2_normalize(k_b.reshape(B, T, H_B, DK_B))
    v_b = v_b.reshape(B, T, H_B, DV_B)
    g = -jnp.exp(a_log.astype(f32)) * jax.nn.softplus(a_b.astype(f32) + dt_bias.astype(f32))
    beta = jax.nn.sigmoid(b_b.astype(f32))
    o_b, new_ssm = gated_delta_rule(q_b, k_b, v_b, g, beta, ssm_state)
    o_b = o_b * lax.rsqrt(jnp.mean(jnp.square(o_b), axis=-1, keepdims=True) + NORM_EPS) * gdn_norm_w.astype(f32)
    o_b = o_b * jax.nn.silu(gate_b.reshape(B, T, H_B, DV_B).astype(f32))
    y_b = o_b.reshape(B, T, H_B * DV_B).astype(x.dtype)

    gate_a, gate_bb = jnp.split(jax.nn.sigmoid((gates + b_gate).astype(f32)), 2, axis=-1)
    mixed = (gate_a * jnp.einsum("bte,ed->btd", y_a, w_proj_a).astype(f32)
             + gate_bb * jnp.einsum("bte,ed->btd", y_b, w_proj_b).astype(f32))
    mix_out = jnp.einsum("btd,de->bte", mixed.astype(x.dtype), w_out)
    x = layer_norm(DEEPNORM_ALPHA * x + mix_out, ln1_g, ln1_b)

    f_gate, f_up = jnp.split(jnp.einsum("btd,df->btf", x, w_up), 2, axis=-1)
    ffn = jnp.einsum("btf,fd->btd", jax.nn.silu(f_gate) * f_up, w_down)
    x = layer_norm(DEEPNORM_ALPHA * x + ffn, ln2_g, ln2_b)
    return x, k_a, v_a, k_i, new_conv, new_ssm


def stack_layer_states(states, i):
    return jnp.stack([s[i] for s in states], axis=0)


def setup_inputs(seed: int = 0) -> dict:
    key = jax.random.key(seed)
    ks = jax.random.split(key, 22)
    f32 = jnp.float32

    def nrm(k, shape, scale):
        return jax.random.normal(k, shape, f32) * scale

    offs = np.cumsum((0,) + IN_SIZES)
    col_scale = np.ones((D_IN,), np.float32)
    col_scale[offs[2]:offs[3]] = DEEPNORM_BETA
    w_in = nrm(ks[7], (DEPTH, D_MODEL, D_IN), D_MODEL ** -0.5) * jnp.asarray(col_scale)
    a_log = jnp.log(jax.random.uniform(ks[10], (DEPTH, H_B), f32, minval=1.0, maxval=16.0))
    dt = jnp.exp(jax.random.uniform(ks[11], (DEPTH, H_B), f32, minval=math.log(1e-3), maxval=math.log(1e-1)))
    dt_bias = dt + jnp.log(-jnp.expm1(-dt))
    return {
        "x_prompt": nrm(ks[0], (BATCH, SEQ, D_MODEL), 1.0),
        "x_sample": nrm(ks[1], (DEC_BATCH, DEC_SEQ, D_MODEL), 1.0),
        "cache_k": nrm(ks[2], (DEPTH, DEC_BATCH, PAST_LEN, KV_A, DH_A), 1.0),
        "cache_v": nrm(ks[3], (DEPTH, DEC_BATCH, PAST_LEN, KV_A, DH_A), DEEPNORM_BETA),
        "cache_kidx": nrm(ks[4], (DEPTH, DEC_BATCH, PAST_LEN, D_IDX), 1.0),
        "state_conv": nrm(ks[5], (DEPTH, DEC_BATCH, CONV_W - 1, CONV_DIM), 1.0),
        "state_ssm": nrm(ks[6], (DEPTH, DEC_BATCH, H_B, DK_B, DV_B), 0.1),
        "w_in": w_in,
        "b_gate": nrm(ks[8], (DEPTH, 2 * D_MODEL), 0.02),
        "conv_w": nrm(ks[9], (DEPTH, CONV_W, CONV_DIM), CONV_W ** -0.5),
        "a_log": a_log,
        "dt_bias": dt_bias,
        "gdn_norm_w": 1.0 + nrm(ks[12], (DEPTH, DV_B), 0.02),
        "w_proj_a": nrm(ks[13], (DEPTH, H_A * DH_A, D_MODEL), (H_A * DH_A) ** -0.5),
        "w_proj_b": nrm(ks[14], (DEPTH, H_B * DV_B, D_MODEL), (H_B * DV_B) ** -0.5),
        "w_out": nrm(ks[15], (DEPTH, D_MODEL, D_MODEL), DEEPNORM_BETA * D_MODEL ** -0.5),
        "ln1_g": 1.0 + nrm(ks[16], (DEPTH, D_MODEL), 0.02),
        "ln1_b": nrm(ks[17], (DEPTH, D_MODEL), 0.02),
        "w_up": nrm(ks[18], (DEPTH, D_MODEL, 2 * D_FF), D_MODEL ** -0.5),
        "w_down": nrm(ks[19], (DEPTH, D_FF, D_MODEL), DEEPNORM_BETA * D_FF ** -0.5),
        "ln2_g": 1.0 + nrm(ks[20], (DEPTH, D_MODEL), 0.02),
        "ln2_b": nrm(ks[21], (DEPTH, D_MODEL), 0.02),
    }


def reference(x_prompt, x_sample, cache_k, cache_v, cache_kidx, state_conv, state_ssm,
              w_in, b_gate, conv_w, a_log, dt_bias, gdn_norm_w, w_proj_a, w_proj_b, w_out,
              ln1_g, ln1_b, w_up, w_down, ln2_g, ln2_b):
    yp = x_prompt
    ys = x_sample
    bp = x_prompt.shape[0]
    dt = x_prompt.dtype
    st_p = []
    st_s = []
    for l in range(DEPTH):
        prm = (w_in[l], b_gate[l], conv_w[l], a_log[l], dt_bias[l], gdn_norm_w[l], w_proj_a[l], w_proj_b[l],
               w_out[l], ln1_g[l], ln1_b[l], w_up[l], w_down[l], ln2_g[l], ln2_b[l])
        empty_kv = jnp.zeros((bp, 0, KV_A, DH_A), dt)
        yp, *sp = trunk_layer(yp, empty_kv, empty_kv, jnp.zeros((bp, 0, D_IDX), dt),
                              jnp.zeros((bp, CONV_W - 1, CONV_DIM), dt),
                              jnp.zeros((bp, H_B, DK_B, DV_B), dt), *prm)
        ys, *ss = trunk_layer(ys, cache_k[l], cache_v[l], cache_kidx[l], state_conv[l], state_ssm[l], *prm)
        st_p.append(sp)
        st_s.append(ss)
    new_k_prompt = stack_layer_states(st_p, 0)
    new_v_prompt = stack_layer_states(st_p, 1)
    new_kidx_prompt = stack_layer_states(st_p, 2)
    new_conv_prompt = stack_layer_states(st_p, 3)
    new_ssm_prompt = stack_layer_states(st_p, 4)
    new_k_sample = stack_layer_states(st_s, 0)
    new_v_sample = stack_layer_states(st_s, 1)
    new_kidx_sample = stack_layer_states(st_s, 2)
    new_conv_sample = stack_layer_states(st_s, 3)
    new_ssm_sample = stack_layer_states(st_s, 4)
    return (yp, ys, new_k_prompt, new_v_prompt, new_kidx_prompt, new_conv_prompt, new_ssm_prompt,
            new_k_sample, new_v_sample, new_kidx_sample, new_conv_sample, new_ssm_sample)
```

```python
import functools
import math

import numpy as np
import jax
import jax.numpy as jnp
from jax import lax
from jax.experimental import pallas as pl
from jax.experimental.pallas import tpu as pltpu

D_MODEL = 1024
DEPTH = 2
CHUNK = 64
H_A = 8
KV_A = 2
DH_A = 64
GROUP_A = H_A // KV_A
H_I = 8
D_IDX = 64
TOPK_MAX = 256
INDEX_SCALE = (H_I * D_IDX) ** -0.5
H_B = 4
DK_B = 128
DV_B = 128
CONV_W = 4
CONV_DIM = 2 * H_B * DK_B + H_B * DV_B
D_FF = int(math.ceil(8 * D_MODEL / 3 / 256)) * 256
DEEPNORM_ALPHA = (2 * DEPTH) ** 0.25
LN_EPS = 1e-5
NORM_EPS = 1e-6
IN_SIZES = (H_A * DH_A, KV_A * DH_A, KV_A * DH_A, H_I * D_IDX, D_IDX, H_I, CONV_DIM, H_B, H_B,
            H_B * DV_B, 2 * D_MODEL)

MXU_DTYPE = jnp.bfloat16
VMEM_LIMIT_BYTES = 56 * 1024 * 1024
LANES = 128
SUBLANES = 8
FLT_MAX = float(np.finfo(np.float32).max)

TOKEN_TILE = 256
KEY_TILE = 256
F_CHUNK = 256

MISC_KIDX = 0
MISC_A = D_IDX + H_I
MISC_B = MISC_A + H_B
MISC_W_COLS = LANES

T_QA = 0
T_QI = H_A * DH_A
T_V = T_QI + H_I * D_IDX
T_W = T_V + KV_A * DH_A
T_ROWS = T_W + 16
ACC_ROWS = DH_A + 16


def _dot(a, b):
    return jnp.dot(a, b, preferred_element_type=jnp.float32)


def _nt_dot(a, b):
    return lax.dot_general(a, b, (((1,), (1,)), ((), ())), preferred_element_type=jnp.float32)


def _tn_dot(a, b):
    return lax.dot_general(a, b, (((0,), (0,)), ((), ())), preferred_element_type=jnp.float32)


def _exact_dot(a, b):
    return jnp.dot(a, b, preferred_element_type=jnp.float32, precision=lax.Precision.HIGHEST)


def _mx(a):
    return a.astype(MXU_DTYPE)


def _sigmoid(x):
    return 1.0 / (1.0 + jnp.exp(-x))


def _params(sem):
    return pltpu.CompilerParams(dimension_semantics=sem, vmem_limit_bytes=VMEM_LIMIT_BYTES)


def _const_spec(shape):
    zeros = (0,) * len(shape)
    return pl.BlockSpec(shape, lambda *_: zeros)


_ROW_GROUPS = (
    ("ka", KV_A * DH_A, jnp.float32),
    ("va", KV_A * DH_A, jnp.float32),
    ("misc", MISC_W_COLS, jnp.float32),
    ("qkvb", CONV_DIM, jnp.float32),
    ("gateb", H_B * DV_B, jnp.float32),
    ("gates", 2 * D_MODEL, jnp.float32),
)


def _arrange_w_in(w_in):
    offs = np.cumsum((0,) + IN_SIZES)
    col = lambda i: w_in[:, offs[i]:offs[i + 1]]
    misc = jnp.concatenate([col(4), col(5), col(7), col(8)], axis=1)
    misc = jnp.pad(misc, ((0, 0), (0, MISC_W_COLS - misc.shape[1])))
    w_rows = jnp.concatenate([col(1), col(2), misc, col(6), col(9), col(10)], axis=1)
    w_cols = jnp.concatenate([col(0), col(3), col(2), col(5)], axis=1).T
    w_cols = jnp.pad(w_cols, ((0, T_ROWS - w_cols.shape[0]), (0, 0)))
    return w_rows, w_cols


def _in_proj_kernel(x_ref, w_ref, wt_ref, ka_ref, va_ref, misc_ref, qkvb_ref, gateb_ref, gates_ref,
                    kb_ref, kidxb_ref, qat_ref, qit_ref, vt_ref, wt_out_ref):
    xb = _mx(x_ref[...])
    off = 0
    for (_, width, _), o_ref in zip(_ROW_GROUPS, (ka_ref, va_ref, misc_ref, qkvb_ref, gateb_ref, gates_ref)):
        for c0 in range(0, width, 512):
            c1 = min(c0 + 512, width)
            o_ref[:, c0:c1] = _dot(xb, w_ref[:, off + c0:off + c1])
        off += width
    kb_ref[...] = _mx(ka_ref[...])
    kidxb_ref[...] = _mx(misc_ref[:, MISC_KIDX:MISC_KIDX + D_IDX])
    qat_ref[0] = _mx(_nt_dot(wt_ref[T_QA:T_QI, :], xb) * (DH_A ** -0.5))
    qit_ref[0] = _mx(_nt_dot(wt_ref[T_QI:T_V, :], xb))
    vt_ref[0] = _mx(_nt_dot(wt_ref[T_V:T_W, :], xb))
    wt_out_ref[0] = _nt_dot(wt_ref[T_W:T_ROWS, :], xb)[:H_I] * INDEX_SCALE


def _in_proj(x2d, w_rows, w_cols):
    n = x2d.shape[0]
    tm = TOKEN_TILE
    nb = n // tm
    row = lambda w: pl.BlockSpec((tm, w), lambda i: (i, 0))
    col = lambda r: pl.BlockSpec((1, r, tm), lambda i: (i, 0, 0))
    out_specs = [row(w) for _, w, _ in _ROW_GROUPS] + [row(KV_A * DH_A), row(D_IDX)] + [
        col(H_A * DH_A), col(H_I * D_IDX), col(KV_A * DH_A), col(H_I)]
    out_shape = [jax.ShapeDtypeStruct((n, w), dt) for _, w, dt in _ROW_GROUPS] + [
        jax.ShapeDtypeStruct((n, KV_A * DH_A), MXU_DTYPE), jax.ShapeDtypeStruct((n, D_IDX), MXU_DTYPE),
        jax.ShapeDtypeStruct((nb, H_A * DH_A, tm), MXU_DTYPE), jax.ShapeDtypeStruct((nb, H_I * D_IDX, tm), MXU_DTYPE),
        jax.ShapeDtypeStruct((nb, KV_A * DH_A, tm), MXU_DTYPE), jax.ShapeDtypeStruct((nb, H_I, tm), jnp.float32)]
    return pl.pallas_call(
        _in_proj_kernel,
        grid=(nb,),
        in_specs=[row(D_MODEL), _const_spec(w_rows.shape), _const_spec(w_cols.shape)],
        out_specs=out_specs,
        out_shape=out_shape,
        compiler_params=_params(("parallel",)),
        name="in_proj",
    )(x2d, w_rows, w_cols)


def _order_key(x):
    bits = lax.bitcast_convert_type(x, jnp.int32)
    return bits ^ ((bits >> 31) & jnp.int32(0x7FFFFFFF))


def _from_order_key(key):
    return lax.bitcast_convert_type(key ^ ((key >> 31) & jnp.int32(0x7FFFFFFF)), jnp.float32)


def _dsa_kernel(qat_ref, qit_ref, wt_ref, k_ref, vt_ref, kidx_ref, o_ref, sc_scr, acc_scr, *,
                tq, tk, past_len, n_keys, n_query, topk):
    iq = pl.program_id(1)
    last_q = past_len + n_query - 1
    q_pos0 = past_len + iq * tq
    q_pos = jnp.minimum(q_pos0 + lax.broadcasted_iota(jnp.int32, (1, tq), 1), last_q)
    q_chunk = q_pos // CHUNK
    n_adm = jnp.minimum((q_chunk + 1) * CHUNK, n_keys)
    n_vis = jnp.minimum((jnp.minimum(q_pos0 + tq - 1, last_q) // CHUNK + 1) * CHUNK, n_keys)
    nk = (n_vis + tk - 1) // tk
    kf = float(topk)

    def key_index(kt):
        return kt * tk + lax.broadcasted_iota(jnp.int32, (tk, 1), 0)

    def fold(x, op):
        return op(x.reshape(tk // (4 * SUBLANES), 4 * SUBLANES, tq), axis=0)

    def score_tile(kt, carry):
        lo, hi = carry
        off = pl.multiple_of(kt * tk, tk)
        kidx_t = kidx_ref[0, pl.ds(off, tk), :]
        score = jnp.zeros((tk, tq), jnp.float32)
        for h in range(H_I):
            z = _dot(kidx_t, qit_ref[0, h * D_IDX:(h + 1) * D_IDX, :])
            score = score + wt_ref[0, h:h + 1, :] * jnp.maximum(z, 0.0)
        s_idx = key_index(kt)
        admissible = (s_idx // CHUNK <= q_chunk) & (s_idx < n_keys)
        masked = jnp.where(admissible, score, -jnp.inf)
        sc_scr[pl.ds(off, tk), :] = masked
        lo = jnp.minimum(lo, fold(jnp.where(admissible, score, jnp.inf), jnp.min))
        hi = jnp.maximum(hi, fold(masked, jnp.max))
        return lo, hi

    init = (jnp.full((4 * SUBLANES, tq), jnp.inf, jnp.float32), jnp.full((4 * SUBLANES, tq), -jnp.inf, jnp.float32))
    s_lo, s_hi = lax.fori_loop(0, nk, score_tile, init)
    s_lo = jnp.min(s_lo, axis=0, keepdims=True)
    s_hi = jnp.max(s_hi, axis=0, keepdims=True)

    def count(pred):
        def body(kt, acc):
            off = pl.multiple_of(kt * tk, tk)
            hit = jnp.where(pred(sc_scr[pl.ds(off, tk), :], key_index(kt)), 1.0, 0.0)
            return acc + fold(hit, jnp.sum)
        acc = lax.fori_loop(0, nk, body, jnp.zeros((4 * SUBLANES, tq), jnp.float32))
        return jnp.sum(acc, axis=0, keepdims=True)

    n_adm_f = n_adm.astype(jnp.float32)
    take_all = n_adm_f <= kf

    def search_cond(st):
        it, _, _, _, _, done = st
        return (it < 80) & (jnp.min(done) < 0.5)

    def search_step(st):
        it, lo_key, hi_key, c_lo, c_hi, done = st
        lo_f = _from_order_key(lo_key)
        hi_f = _from_order_key(hi_key)
        interp = lo_f + (hi_f - lo_f) * ((c_lo - kf) / (c_lo - c_hi))
        middle = 0.5 * lo_f + 0.5 * hi_f
        by_value = _order_key(jnp.where(it % 2 == 0, interp, middle))
        by_key = (lo_key >> 1) + (hi_key >> 1) + ((lo_key | hi_key) & 1)
        cand_key = jnp.where(it < 40, by_value, by_key)
        cand_key = jnp.minimum(jnp.maximum(cand_key, lo_key + 1), hi_key)
        cand = _from_order_key(cand_key)
        c = count(lambda x, _: x >= cand)
        active = done < 0.5
        up = active & (c >= kf)
        down = active & (c < kf)
        lo_key = jnp.where(up, cand_key, lo_key)
        c_lo = jnp.where(up, c, c_lo)
        hi_key = jnp.where(down, cand_key - 1, hi_key)
        c_hi = jnp.where(down, c, c_hi)
        done = jnp.where((c_lo == kf) | (lo_key >= hi_key), 1.0, done)
        return it + 1, lo_key, hi_key, c_lo, c_hi, done

    lo_key0 = _order_key(s_lo)
    hi_key0 = _order_key(s_hi)
    done0 = jnp.where(take_all | (lo_key0 >= hi_key0), 1.0, 0.0)
    st = (jnp.int32(0), lo_key0, hi_key0, n_adm_f, jnp.zeros((1, tq), jnp.float32), done0)
    _, lo_key, _, c_lo, c_hi, _ = lax.while_loop(search_cond, search_step, st)
    thr = jnp.where(take_all, -FLT_MAX, _from_order_key(lo_key))

    excess = (c_lo > kf) & jnp.logical_not(take_all)
    n_rows = sc_scr.shape[0]

    @pl.when(jnp.max(jnp.where(excess, 1.0, 0.0)) > 0.5)
    def _():
        need = kf - c_hi
        n_bits = max(1, int(n_rows - 1).bit_length())

        def idx_step(i, bound):
            cand = bound | lax.shift_left(jnp.int32(1), n_bits - 1 - i)
            below = count(lambda x, idx: (x == thr) & (idx < cand))
            return jnp.where(below < need, cand, bound)

        bound = lax.fori_loop(0, n_bits, idx_step, jnp.zeros((1, tq), jnp.int32))
        bound = jnp.where(excess, bound, jnp.int32(n_rows))

        def drop_tile(kt, carry):
            off = pl.multiple_of(kt * tk, tk)
            x = sc_scr[pl.ds(off, tk), :]
            sc_scr[pl.ds(off, tk), :] = jnp.where((x == thr) & (key_index(kt) > bound), -jnp.inf, x)
            return carry

        lax.fori_loop(0, nk, drop_tile, 0)

    acc_scr[...] = jnp.zeros(acc_scr.shape, jnp.float32)
    ones_rows = jnp.ones((ACC_ROWS - DH_A, tk), MXU_DTYPE)

    def attend_tile(kt, m_all):
        off = pl.multiple_of(kt * tk, tk)
        keep = sc_scr[pl.ds(off, tk), :] >= thr
        m_out = []
        for g in range(KV_A):
            heads = range(g * GROUP_A, (g + 1) * GROUP_A)
            k_t = k_ref[0, pl.ds(off, tk), g * DH_A:(g + 1) * DH_A]
            vt_ext = jnp.concatenate([vt_ref[0, kt, g * DH_A:(g + 1) * DH_A, :], ones_rows], axis=0)
            s = [jnp.where(keep, _dot(k_t, qat_ref[0, h * DH_A:(h + 1) * DH_A, :]), -jnp.inf) for h in heads]
            m_new = [jnp.maximum(m_all[h], jnp.max(s_h, axis=0, keepdims=True)) for h, s_h in zip(heads, s)]
            m_safe = [jnp.where(m == -jnp.inf, 0.0, m) for m in m_new]
            p = [_mx(jnp.exp(s_h - m)) for s_h, m in zip(s, m_safe)]
            pv = [_dot(vt_ext, p_h) for p_h in p]
            for h, m, pv_h in zip(heads, m_safe, pv):
                acc_scr[h] = jnp.exp(m_all[h] - m) * acc_scr[h] + pv_h
            m_out += m_new
        return tuple(m_out)

    m_init = tuple(jnp.full((1, tq), -jnp.inf, jnp.float32) for _ in range(H_A))
    lax.fori_loop(0, nk, attend_tile, m_init)
    outs = [acc_scr[h, :DH_A, :] / acc_scr[h, DH_A:DH_A + 1, :] for h in range(H_A)]
    o_ref[0] = jnp.concatenate(outs, axis=0).T.astype(o_ref.dtype)


def _dsa_attention(qat, qit, wt, k_all, vt_all, kidx_all, *, batch, past_len, n_keys, n_query):
    tq = qat.shape[2]
    nq = qat.shape[0] // batch
    lp = k_all.shape[1]
    tk = KEY_TILE
    topk = min(TOPK_MAX, n_keys // 4)
    kern = functools.partial(_dsa_kernel, tq=tq, tk=tk, past_len=past_len, n_keys=n_keys, n_query=n_query, topk=topk)
    qspec = lambda rows: pl.BlockSpec((1, rows, tq), lambda i, j: (i * nq + j, 0, 0))
    return pl.pallas_call(
        kern,
        grid=(batch, nq),
        in_specs=[
            qspec(H_A * DH_A), qspec(H_I * D_IDX), qspec(H_I),
            pl.BlockSpec((1, lp, KV_A * DH_A), lambda i, j: (i, 0, 0)),
            pl.BlockSpec((1, lp // tk, KV_A * DH_A, tk), lambda i, j: (i, 0, 0, 0)),
            pl.BlockSpec((1, lp, D_IDX), lambda i, j: (i, 0, 0)),
        ],
        out_specs=pl.BlockSpec((1, tq, H_A * DH_A), lambda i, j: (i, j, 0)),
        out_shape=jax.ShapeDtypeStruct((batch, nq * tq, H_A * DH_A), MXU_DTYPE),
        scratch_shapes=[pltpu.VMEM((lp, tq), jnp.float32), pltpu.VMEM((H_A, ACC_ROWS, tq), jnp.float32)],
        compiler_params=_params(("parallel", "arbitrary")),
        name="dsa_attention",
    )(qat, qit, wt, k_all, vt_all, kidx_all)


def _softplus(x):
    return jnp.maximum(x, 0.0) + jnp.log(1.0 + jnp.exp(-jnp.abs(x)))


def _unit_lower_inverses(a_strict, c):
    eye = (lax.broadcasted_iota(jnp.int32, (c, c), 0) == lax.broadcasted_iota(jnp.int32, (c, c), 1)
           ).astype(jnp.float32)
    power = [-a for a in a_strict]
    inv = [eye + p for p in power]
    span = 2
    while span < c:
        pb = [_mx(p) for p in power]
        power = [_dot(p, p) for p in pb]
        inv = [i + _dot(_mx(i), _mx(p)) for i, p in zip(inv, power)]
        span *= 2
    return inv


def _gdn_kernel(qkvb_ref, gateb_ref, misc_ref, a_t_ref, conv0_ref, ssm0_ref, convw_ref,
                alog_r_ref, dtb_r_ref, alog_c_ref, dtb_c_ref, normw_ref,
                y_ref, ssm_ref, tail_scr, *, c, n_chunks):
    rows = c * n_chunks

    @pl.when(pl.program_id(1) == 0)
    def _():
        tail_scr[...] = conv0_ref[0]
        ssm_ref[...] = ssm0_ref[...]

    x = qkvb_ref[0]
    ext = jnp.concatenate([tail_scr[...], x], axis=0)
    conv = None
    for j in range(CONV_W):
        shifted = ext if j == CONV_W - 1 else pltpu.roll(ext, CONV_W - 1 - j, axis=0)
        term = shifted[SUBLANES:SUBLANES + rows] * convw_ref[j:j + 1, :]
        conv = term if conv is None else conv + term
    tail_scr[...] = x[rows - SUBLANES:rows]
    act = conv * _sigmoid(conv)

    misc = misc_ref[0]
    beta_all = _sigmoid(misc)
    g_col = -jnp.exp(alog_r_ref[...]) * _softplus(misc + dtb_r_ref[...])
    ri = lax.broadcasted_iota(jnp.int32, (rows, rows), 0)
    ci = lax.broadcasted_iota(jnp.int32, (rows, rows), 1)
    same_chunk_causal = ((ci <= ri) & (ci // c == ri // c)).astype(jnp.float32)
    big_g_col = _exact_dot(same_chunk_causal, g_col)
    upper = (lax.broadcasted_iota(jnp.int32, (LANES, LANES), 0)
             <= lax.broadcasted_iota(jnp.int32, (LANES, LANES), 1)).astype(jnp.float32)
    rc = lax.broadcasted_iota(jnp.int32, (c, c), 0)
    cc = lax.broadcasted_iota(jnp.int32, (c, c), 1)
    causal = cc <= rc
    strict = cc < rc

    big_g_row = [_exact_dot(-jnp.exp(alog_c_ref[...]) * _softplus(a_t_ref[0, n] + dtb_c_ref[...]), upper)
                 for n in range(n_chunks)]
    pairs = [(n, h) for n in range(n_chunks) for h in range(H_B)]
    tile = lambda a, n, col, width: a[n * c:(n + 1) * c, col:col + width]
    l2n = lambda a: a * lax.rsqrt(jnp.sum(a * a, axis=1, keepdims=True) + NORM_EPS)
    q = [l2n(tile(act, n, h * DK_B, DK_B)) * (DK_B ** -0.5) for n, h in pairs]
    k = [l2n(tile(act, n, (H_B + h) * DK_B, DK_B)) for n, h in pairs]
    v = [tile(act, n, 2 * H_B * DK_B + h * DV_B, DV_B) for n, h in pairs]
    gc = [tile(big_g_col, n, MISC_A + h, 1) for n, h in pairs]
    gr = [big_g_row[n][h:h + 1, :c] for n, h in pairs]
    beta = [tile(beta_all, n, MISC_B + h, 1) for n, h in pairs]
    g_last = [g[c - 1:c, :] for g in gc]
    exp_g = [jnp.exp(g) for g in gc]
    decay = [jnp.where(causal, jnp.exp(jnp.where(causal, a - b, 0.0)), 0.0) for a, b in zip(gc, gr)]
    kb = [_mx(a) for a in k]
    kk = [_nt_dot(a, a) for a in kb]
    qk = [_nt_dot(_mx(a), b) for a, b in zip(q, kb)]
    a_mat = [jnp.where(strict, b * m * d, 0.0) for b, m, d in zip(beta, kk, decay)]
    inv = _unit_lower_inverses(a_mat, c)
    rhs = [_mx(jnp.concatenate([vv * b, kx * (b * e)], axis=1)) for vv, kx, b, e in zip(v, k, beta, exp_g)]
    sol = [_dot(_mx(i), r) for i, r in zip(inv, rhs)]
    u = [s[:, :DV_B] for s in sol]
    w_qd = [_mx(jnp.concatenate([s[:, DV_B:], a * e], axis=0)) for s, a, e in zip(sol, q, exp_g)]
    qk = [_mx(m * d) for m, d in zip(qk, decay)]
    k_dec = [_mx(a * jnp.exp(gl - g)) for a, gl, g in zip(k, g_last, gc)]
    g_tot = [jnp.exp(gl) for gl in g_last]

    for n in range(n_chunks):
        ids = [n * H_B + h for h in range(H_B)]
        state = [ssm_ref[0, h] for h in range(H_B)]
        both = [_dot(w_qd[i], _mx(s)) for i, s in zip(ids, state)]
        v_new = [_mx(u[i] - b[:c]) for i, b in zip(ids, both)]
        delta = [_tn_dot(k_dec[i], vn) for i, vn in zip(ids, v_new)]
        for h, i in enumerate(ids):
            ssm_ref[0, h] = state[h] * g_tot[i] + delta[h]
        o = [b[c:] + _dot(qk[i], vn) for i, b, vn in zip(ids, both, v_new)]
        for h in range(H_B):
            oh = o[h] * lax.rsqrt(jnp.mean(o[h] * o[h], axis=1, keepdims=True) + NORM_EPS) * normw_ref[...]
            gate = gateb_ref[0, n * c:(n + 1) * c, h * DV_B:(h + 1) * DV_B]
            y_ref[0, n * c:(n + 1) * c, h * DV_B:(h + 1) * DV_B] = (oh * (gate * _sigmoid(gate))).astype(y_ref.dtype)


def _gated_delta(qkvb, gateb, misc, a_t, conv0, ssm0, conv_w, a_log, dt_bias, norm_w):
    b, t, _ = qkvb.shape
    c = min(CHUNK, t)
    n_chunks = max(1, min(4, t // c))
    rows = c * n_chunks
    in_slab = lambda v: jnp.pad(v, (MISC_A, MISC_W_COLS - MISC_A - H_B)).reshape(1, MISC_W_COLS)
    in_rows = lambda v: jnp.pad(v, (0, SUBLANES - H_B)).reshape(SUBLANES, 1)
    small = [in_slab(a_log), in_slab(dt_bias), in_rows(a_log), in_rows(dt_bias), norm_w.reshape(1, DV_B)]
    return pl.pallas_call(
        functools.partial(_gdn_kernel, c=c, n_chunks=n_chunks),
        grid=(b, t // rows),
        in_specs=[
            pl.BlockSpec((1, rows, CONV_DIM), lambda i, j: (i, j, 0)),
            pl.BlockSpec((1, rows, H_B * DV_B), lambda i, j: (i, j, 0)),
            pl.BlockSpec((1, rows, MISC_W_COLS), lambda i, j: (i, j, 0)),
            pl.BlockSpec((1, n_chunks, SUBLANES, LANES), lambda i, j: (i, j, 0, 0)),
            pl.BlockSpec((1, SUBLANES, CONV_DIM), lambda i, j: (i, 0, 0)),
            pl.BlockSpec((1, H_B, DK_B, DV_B), lambda i, j: (i, 0, 0, 0)),
            _const_spec(conv_w.shape),
        ] + [_const_spec(s.shape) for s in small],
        out_specs=[
            pl.BlockSpec((1, rows, H_B * DV_B), lambda i, j: (i, j, 0)),
            pl.BlockSpec((1, H_B, DK_B, DV_B), lambda i, j: (i, 0, 0, 0)),
        ],
        out_shape=[jax.ShapeDtypeStruct((b, t, H_B * DV_B), MXU_DTYPE),
                   jax.ShapeDtypeStruct((b, H_B, DK_B, DV_B), jnp.float32)],
        scratch_shapes=[pltpu.VMEM((SUBLANES, CONV_DIM), jnp.float32)],
        compiler_params=_params(("parallel", "arbitrary")),
        name="gated_delta",
    )(qkvb, gateb, misc, a_t, conv0, ssm0, conv_w, *small)


def _layer_norm(x, g, b):
    mu = jnp.mean(x, axis=-1, keepdims=True)
    xc = x - mu
    var = jnp.mean(xc * xc, axis=-1, keepdims=True)
    return xc * lax.rsqrt(var + LN_EPS) * g + b


def _merge_kernel(x_ref, gates_ref, ya_ref, yb_ref, bgate_ref, wpa_ref, wpb_ref, wout_ref, g_ref, b_ref, o_ref):
    gates = _sigmoid(gates_ref[...] + bgate_ref[...])
    mixed = (gates[:, :D_MODEL] * _dot(ya_ref[...], wpa_ref[...])
             + gates[:, D_MODEL:] * _dot(yb_ref[...], wpb_ref[...]))
    mix_out = _dot(_mx(mixed), wout_ref[...])
    o_ref[...] = _layer_norm(DEEPNORM_ALPHA * x_ref[...] + mix_out, g_ref[...], b_ref[...])


def _merge(x2d, gates, ya, yb, b_gate, wpa, wpb, wout, ln_g, ln_b):
    n = x2d.shape[0]
    tm = TOKEN_TILE
    row = lambda w: pl.BlockSpec((tm, w), lambda i: (i, 0))
    consts = [b_gate.reshape(1, -1), wpa, wpb, wout, ln_g.reshape(1, -1), ln_b.reshape(1, -1)]
    return pl.pallas_call(
        _merge_kernel,
        grid=(n // tm,),
        in_specs=[row(D_MODEL), row(2 * D_MODEL), row(H_A * DH_A), row(H_B * DV_B)]
        + [_const_spec(c.shape) for c in consts],
        out_specs=row(D_MODEL),
        out_shape=jax.ShapeDtypeStruct((n, D_MODEL), jnp.float32),
        compiler_params=_params(("parallel",)),
        name="merge_out_ln",
    )(x2d, gates, ya, yb, *consts)


def _ffn_kernel(x_ref, wup_ref, wdown_ref, g_ref, b_ref, o_ref):
    x = x_ref[...]
    xb = _mx(x)
    acc = jnp.zeros(x.shape, jnp.float32)
    for c0 in range(0, D_FF, F_CHUNK):
        f_gate = _dot(xb, wup_ref[:, c0:c0 + F_CHUNK])
        f_up = _dot(xb, wup_ref[:, D_FF + c0:D_FF + c0 + F_CHUNK])
        hidden = f_gate * _sigmoid(f_gate) * f_up
        acc = acc + _dot(_mx(hidden), wdown_ref[c0:c0 + F_CHUNK, :])
    o_ref[...] = _layer_norm(DEEPNORM_ALPHA * x + acc, g_ref[...], b_ref[...])


def _ffn(x2d, w_up, w_down, ln_g, ln_b):
    n = x2d.shape[0]
    tm = TOKEN_TILE
    row = pl.BlockSpec((tm, D_MODEL), lambda i: (i, 0))
    consts = [w_up, w_down, ln_g.reshape(1, -1), ln_b.reshape(1, -1)]
    return pl.pallas_call(
        _ffn_kernel,
        grid=(n // tm,),
        in_specs=[row] + [_const_spec(c.shape) for c in consts],
        out_specs=row,
        out_shape=jax.ShapeDtypeStruct((n, D_MODEL), jnp.float32),
        compiler_params=_params(("parallel",)),
        name="swiglu_ln",
    )(x2d, *consts)


def _round_up(a, m):
    return (a + m - 1) // m * m


def _trunk_layer(x, past_k, past_v, past_kidx, conv_state, ssm_state, prm):
    (w_rows, w_cols, b_gate, conv_w, a_log, dt_bias, norm_w, wpa, wpb, wout, ln1_g, ln1_b, w_up, w_down, ln2_g,
     ln2_b) = prm
    b, t, _ = x.shape
    n = b * t
    assert n % TOKEN_TILE == 0
    past_len = past_k.shape[1]
    n_keys = past_len + t
    x2d = x.reshape(n, D_MODEL)
    ka, va, misc, qkvb, gateb, gates, kb, kidxb, qat, qit, vt, wt = _in_proj(x2d, w_rows, w_cols)

    k_new = ka.reshape(b, t, KV_A, DH_A)
    v_new = va.reshape(b, t, KV_A, DH_A)
    kidx_new = misc[:, MISC_KIDX:MISC_KIDX + D_IDX].reshape(b, t, D_IDX)
    lp = _round_up(n_keys, KEY_TILE)
    if past_len == 0 and t % TOKEN_TILE == 0:
        k_all = kb.reshape(b, t, KV_A * DH_A)
        kidx_all = kidxb.reshape(b, t, D_IDX)
        vt_all = vt.reshape(b, t // KEY_TILE, KV_A * DH_A, KEY_TILE)
        n_query, q_lanes = t, TOKEN_TILE
    else:
        assert t <= LANES and TOKEN_TILE % t == 0
        pad_keys = lambda a: jnp.pad(a, ((0, 0), (0, lp - n_keys), (0, 0)))
        k_all = pad_keys(jnp.concatenate([_mx(past_k).reshape(b, past_len, -1), kb.reshape(b, t, -1)], axis=1))
        kidx_all = pad_keys(jnp.concatenate([_mx(past_kidx), kidxb.reshape(b, t, -1)], axis=1))
        v_all = pad_keys(jnp.concatenate([_mx(past_v).reshape(b, past_len, -1), _mx(va).reshape(b, t, -1)], axis=1))
        vt_all = jnp.transpose(v_all.reshape(b, lp // KEY_TILE, KEY_TILE, KV_A * DH_A), (0, 1, 3, 2))

        def per_batch(a):
            f = a.shape[1]
            a = jnp.transpose(a.reshape(-1, f, TOKEN_TILE // t, t), (0, 2, 1, 3)).reshape(b, f, t)
            return jnp.pad(a, ((0, 0), (0, 0), (0, LANES - t)))

        qat, qit, wt = per_batch(qat), per_batch(qit), per_batch(wt)
        n_query, q_lanes = t, LANES
    y_a = _dsa_attention(qat, qit, wt, k_all, vt_all, kidx_all, batch=b, past_len=past_len, n_keys=n_keys,
                         n_query=n_query)
    y_a = y_a[:, :t].reshape(n, H_A * DH_A)
    del q_lanes

    c = min(CHUNK, t)
    a_t = misc[:, MISC_A:MISC_A + SUBLANES].reshape(b, t // c, c, SUBLANES)
    a_t = jnp.pad(jnp.transpose(a_t, (0, 1, 3, 2)), ((0, 0), (0, 0), (0, 0), (0, LANES - c)))
    qkvb3 = qkvb.reshape(b, t, CONV_DIM)
    conv0 = jnp.pad(conv_state, ((0, 0), (SUBLANES - (CONV_W - 1), 0), (0, 0)))
    y_b, new_ssm = _gated_delta(qkvb3, gateb.reshape(b, t, -1), misc.reshape(b, t, -1), a_t, conv0,
                                ssm_state, conv_w, a_log, dt_bias, norm_w)
    new_conv = jnp.concatenate([conv_state, qkvb3], axis=1)[:, t:]

    x1 = _merge(x2d, gates, y_a, y_b.reshape(n, -1), b_gate, wpa, wpb, wout, ln1_g, ln1_b)
    x2 = _ffn(x1, w_up, w_down, ln2_g, ln2_b)
    return x2.reshape(b, t, D_MODEL), k_new, v_new, kidx_new, new_conv, new_ssm


def _layer_params(l, w_in, b_gate, conv_w, a_log, dt_bias, gdn_norm_w, w_proj_a, w_proj_b, w_out, ln1_g, ln1_b,
                  w_up, w_down, ln2_g, ln2_b):
    w_rows, w_cols = _arrange_w_in(w_in[l])
    return (_mx(w_rows), _mx(w_cols), b_gate[l], conv_w[l], a_log[l], dt_bias[l], gdn_norm_w[l],
            _mx(w_proj_a[l]), _mx(w_proj_b[l]), _mx(w_out[l]), ln1_g[l], ln1_b[l],
            _mx(w_up[l]), _mx(w_down[l]), ln2_g[l], ln2_b[l])


def kernel(x_prompt, x_sample, cache_k, cache_v, cache_kidx, state_conv, state_ssm, w_in, b_gate, conv_w, a_log,
           dt_bias, gdn_norm_w, w_proj_a, w_proj_b, w_out, ln1_g, ln1_b, w_up, w_down, ln2_g, ln2_b):
    yp, ys = x_prompt, x_sample
    bp = x_prompt.shape[0]
    dt = x_prompt.dtype
    st_p, st_s = [], []
    for l in range(DEPTH):
        prm = _layer_params(l, w_in, b_gate, conv_w, a_log, dt_bias, gdn_norm_w, w_proj_a, w_proj_b, w_out,
                            ln1_g, ln1_b, w_up, w_down, ln2_g, ln2_b)
        empty_kv = jnp.zeros((bp, 0, KV_A, DH_A), dt)
        yp, *sp = _trunk_layer(yp, empty_kv, empty_kv, jnp.zeros((bp, 0, D_IDX), dt),
                               jnp.zeros((bp, CONV_W - 1, CONV_DIM), dt),
                               jnp.zeros((bp, H_B, DK_B, DV_B), dt), prm)
        ys, *ss = _trunk_layer(ys, cache_k[l], cache_v[l], cache_kidx[l], state_conv[l], state_ssm[l], prm)
        st_p.append(sp)
        st_s.append(ss)
    stack = lambda states, i: jnp.stack([s[i] for s in states], axis=0)
    return (yp, ys) + tuple(stack(st_p, i) for i in range(5)) + tuple(stack(st_s, i) for i in range(5))
```

```python
import functools
import math

import numpy as np
import jax
import jax.numpy as jnp
from jax import lax
from jax.experimental import pallas as pl
from jax.experimental.pallas import tpu as pltpu

D_MODEL = 1024
DEPTH = 2
CHUNK = 64
H_A = 8
KV_A = 2
DH_A = 64
GROUP_A = H_A // KV_A
H_I = 8
D_IDX = 64
TOPK_MAX = 256
INDEX_SCALE = (H_I * D_IDX) ** -0.5
H_B = 4
DK_B = 128
DV_B = 128
CONV_W = 4
CONV_DIM = 2 * H_B * DK_B + H_B * DV_B
D_FF = int(math.ceil(8 * D_MODEL / 3 / 256)) * 256
DEEPNORM_ALPHA = (2 * DEPTH) ** 0.25
LN_EPS = 1e-5
NORM_EPS = 1e-6
IN_SIZES = (H_A * DH_A, KV_A * DH_A, KV_A * DH_A, H_I * D_IDX, D_IDX, H_I, CONV_DIM, H_B, H_B,
            H_B * DV_B, 2 * D_MODEL)

MXU_DTYPE = jnp.bfloat16
VMEM_LIMIT_BYTES = 56 * 1024 * 1024
LANES = 128
SUBLANES = 8
FLT_MAX = float(np.finfo(np.float32).max)

TOKEN_TILE = 256
KEY_TILE = 256
PACKED_KEY_TILE = 512
F_CHUNK = 256

MISC_KIDX = 0
MISC_A = D_IDX + H_I
MISC_B = MISC_A + H_B
MISC_W_COLS = LANES

T_QA = 0
T_QI = H_A * DH_A
T_V = T_QI + H_I * D_IDX
T_W = T_V + KV_A * DH_A
T_ROWS = T_W + 16
ACC_ROWS = DH_A + 16


def _dot(a, b):
    return jnp.dot(a, b, preferred_element_type=jnp.float32)


def _nt_dot(a, b):
    return lax.dot_general(a, b, (((1,), (1,)), ((), ())), preferred_element_type=jnp.float32)


def _tn_dot(a, b):
    return lax.dot_general(a, b, (((0,), (0,)), ((), ())), preferred_element_type=jnp.float32)


def _exact_dot(a, b):
    return jnp.dot(a, b, preferred_element_type=jnp.float32, precision=lax.Precision.HIGHEST)


def _mx(a):
    return a.astype(MXU_DTYPE)


def _sigmoid(x):
    return 1.0 / (1.0 + jnp.exp(-x))


def _params(sem):
    return pltpu.CompilerParams(dimension_semantics=sem, vmem_limit_bytes=VMEM_LIMIT_BYTES)


def _const_spec(shape):
    zeros = (0,) * len(shape)
    return pl.BlockSpec(shape, lambda *_: zeros)


_ROW_GROUPS = (
    ("ka", KV_A * DH_A, jnp.float32),
    ("va", KV_A * DH_A, jnp.float32),
    ("misc", MISC_W_COLS, jnp.float32),
    ("qkvb", CONV_DIM, jnp.float32),
    ("gateb", H_B * DV_B, jnp.float32),
    ("gates", 2 * D_MODEL, jnp.float32),
)


def _arrange_w_in(w_in):
    offs = np.cumsum((0,) + IN_SIZES)
    col = lambda i: w_in[:, offs[i]:offs[i + 1]]
    misc = jnp.concatenate([col(4), col(5), col(7), col(8)], axis=1)
    misc = jnp.pad(misc, ((0, 0), (0, MISC_W_COLS - misc.shape[1])))
    w_rows = jnp.concatenate([col(1), col(2), misc, col(6), col(9), col(10)], axis=1)
    w_cols = jnp.concatenate([col(0), col(3), col(2), col(5)], axis=1).T
    w_cols = jnp.pad(w_cols, ((0, T_ROWS - w_cols.shape[0]), (0, 0)))
    return w_rows, w_cols


def _in_proj_kernel(x_ref, w_ref, wt_ref, ka_ref, va_ref, misc_ref, qkvb_ref, gateb_ref, gates_ref,
                    kb_ref, kidxb_ref, qat_ref, qit_ref, vt_ref, wt_out_ref):
    xb = _mx(x_ref[...])
    off = 0
    for (_, width, _), o_ref in zip(_ROW_GROUPS, (ka_ref, va_ref, misc_ref, qkvb_ref, gateb_ref, gates_ref)):
        for c0 in range(0, width, 512):
            c1 = min(c0 + 512, width)
            o_ref[:, c0:c1] = _dot(xb, w_ref[:, off + c0:off + c1])
        off += width
    kb_ref[...] = _mx(ka_ref[...])
    kidxb_ref[...] = _mx(misc_ref[:, MISC_KIDX:MISC_KIDX + D_IDX])
    qat_ref[0] = _mx(_nt_dot(wt_ref[T_QA:T_QI, :], xb) * (DH_A ** -0.5))
    qit_ref[0] = _mx(_nt_dot(wt_ref[T_QI:T_V, :], xb))
    vt_ref[0] = _mx(_nt_dot(wt_ref[T_V:T_W, :], xb))
    wt_out_ref[0] = _nt_dot(wt_ref[T_W:T_ROWS, :], xb)[:H_I] * INDEX_SCALE


def _in_proj(x2d, w_rows, w_cols):
    n = x2d.shape[0]
    tm = TOKEN_TILE
    nb = n // tm
    row = lambda w: pl.BlockSpec((tm, w), lambda i: (i, 0))
    col = lambda r: pl.BlockSpec((1, r, tm), lambda i: (i, 0, 0))
    out_specs = [row(w) for _, w, _ in _ROW_GROUPS] + [row(KV_A * DH_A), row(D_IDX)] + [
        col(H_A * DH_A), col(H_I * D_IDX), col(KV_A * DH_A), col(H_I)]
    out_shape = [jax.ShapeDtypeStruct((n, w), dt) for _, w, dt in _ROW_GROUPS] + [
        jax.ShapeDtypeStruct((n, KV_A * DH_A), MXU_DTYPE), jax.ShapeDtypeStruct((n, D_IDX), MXU_DTYPE),
        jax.ShapeDtypeStruct((nb, H_A * DH_A, tm), MXU_DTYPE), jax.ShapeDtypeStruct((nb, H_I * D_IDX, tm), MXU_DTYPE),
        jax.ShapeDtypeStruct((nb, KV_A * DH_A, tm), MXU_DTYPE), jax.ShapeDtypeStruct((nb, H_I, tm), jnp.float32)]
    return pl.pallas_call(
        _in_proj_kernel,
        grid=(nb,),
        in_specs=[row(D_MODEL), _const_spec(w_rows.shape), _const_spec(w_cols.shape)],
        out_specs=out_specs,
        out_shape=out_shape,
        compiler_params=_params(("parallel",)),
        name="in_proj",
    )(x2d, w_rows, w_cols)


def _order_key(x):
    bits = lax.bitcast_convert_type(x, jnp.int32)
    return bits ^ ((bits >> 31) & jnp.int32(0x7FFFFFFF))


def _from_order_key(key):
    return lax.bitcast_convert_type(key ^ ((key >> 31) & jnp.int32(0x7FFFFFFF)), jnp.float32)


FOLD_ROWS = 4 * SUBLANES


def _fold(x, op):
    return op(x.reshape(x.shape[0] // FOLD_ROWS, FOLD_ROWS, x.shape[1]), axis=0)


def _key_index(kt, tk):
    return kt * tk + lax.broadcasted_iota(jnp.int32, (tk, 1), 0)


def _select_threshold(sc_scr, nk, tk, n_adm, s_lo, s_hi, topk):
    lanes = sc_scr.shape[1]
    kf = float(topk)

    def count(pred):
        def body(kt, acc):
            off = pl.multiple_of(kt * tk, tk)
            hit = jnp.where(pred(sc_scr[pl.ds(off, tk), :], _key_index(kt, tk)), 1.0, 0.0)
            return acc + _fold(hit, jnp.sum)
        acc = lax.fori_loop(0, nk, body, jnp.zeros((FOLD_ROWS, lanes), jnp.float32))
        return jnp.sum(acc, axis=0, keepdims=True)

    n_adm_f = n_adm.astype(jnp.float32)
    take_all = n_adm_f <= kf

    def search_step(st):
        it, lo_key, hi_key, c_lo, c_hi, done = st
        lo_f = _from_order_key(lo_key)
        hi_f = _from_order_key(hi_key)
        interp = lo_f + (hi_f - lo_f) * ((c_lo - kf) / (c_lo - c_hi))
        middle = 0.5 * lo_f + 0.5 * hi_f
        by_value = _order_key(jnp.where(it % 2 == 0, interp, middle))
        by_key = (lo_key >> 1) + (hi_key >> 1) + ((lo_key | hi_key) & 1)
        cand_key = jnp.where(it < 2, it, jnp.where(it < 40, by_value, by_key))
        cand_key = jnp.minimum(jnp.maximum(cand_key, lo_key + 1), hi_key)
        cand = _from_order_key(cand_key)
        c = count(lambda x, _: x >= cand)
        active = done < 0.5
        up = active & (c >= kf)
        down = active & (c < kf)
        lo_key = jnp.where(up, cand_key, lo_key)
        c_lo = jnp.where(up, c, c_lo)
        hi_key = jnp.where(down, cand_key - 1, hi_key)
        c_hi = jnp.where(down, c, c_hi)
        done = jnp.where((c_lo == kf) | (lo_key >= hi_key), 1.0, done)
        return it + 1, lo_key, hi_key, c_lo, c_hi, done

    def search_cond(st):
        return (st[0] < 80) & (jnp.min(st[5]) < 0.5)

    lo_key0 = _order_key(s_lo)
    hi_key0 = _order_key(s_hi)
    done0 = jnp.where(take_all | (lo_key0 >= hi_key0), 1.0, 0.0)
    st = (jnp.int32(0), lo_key0, hi_key0, n_adm_f, jnp.zeros((1, lanes), jnp.float32), done0)
    _, lo_key, _, c_lo, c_hi, _ = lax.while_loop(search_cond, lambda s: search_step(search_step(s)), st)
    thr = jnp.where(take_all, -FLT_MAX, _from_order_key(lo_key))

    excess = (c_lo > kf) & jnp.logical_not(take_all)
    n_rows = sc_scr.shape[0]

    @pl.when(jnp.max(jnp.where(excess, 1.0, 0.0)) > 0.5)
    def _():
        need = kf - c_hi
        n_bits = max(1, int(n_rows - 1).bit_length())

        def idx_step(i, bound):
            cand = bound | lax.shift_left(jnp.int32(1), n_bits - 1 - i)
            below = count(lambda x, idx: (x == thr) & (idx < cand))
            return jnp.where(below < need, cand, bound)

        bound = lax.fori_loop(0, n_bits, idx_step, jnp.zeros((1, lanes), jnp.int32))
        bound = jnp.where(excess, bound, jnp.int32(n_rows))

        def drop_tile(kt, carry):
            off = pl.multiple_of(kt * tk, tk)
            x = sc_scr[pl.ds(off, tk), :]
            sc_scr[pl.ds(off, tk), :] = jnp.where((x == thr) & (_key_index(kt, tk) > bound), -jnp.inf, x)
            return carry

        lax.fori_loop(0, nk, drop_tile, 0)

    return thr


def _store_scores(sc_scr, off, tk, score, admissible, carry):
    lo, hi = carry
    masked = jnp.where(admissible, score, -jnp.inf)
    sc_scr[pl.ds(off, tk), :] = masked
    lo = jnp.minimum(lo, _fold(jnp.where(admissible, score, jnp.inf), jnp.min))
    hi = jnp.maximum(hi, _fold(masked, jnp.max))
    return lo, hi


def _score_range_init(lanes):
    return (jnp.full((FOLD_ROWS, lanes), jnp.inf, jnp.float32), jnp.full((FOLD_ROWS, lanes), -jnp.inf, jnp.float32))


def _dsa_kernel(qat_ref, qit_ref, wt_ref, k_ref, vt_ref, kidx_ref, o_ref, sc_scr, acc_scr, *,
                tq, tk, past_len, n_keys, n_query, topk):
    iq = pl.program_id(1)
    last_q = past_len + n_query - 1
    q_pos0 = past_len + iq * tq
    q_pos = jnp.minimum(q_pos0 + lax.broadcasted_iota(jnp.int32, (1, tq), 1), last_q)
    q_chunk = q_pos // CHUNK
    n_adm = jnp.minimum((q_chunk + 1) * CHUNK, n_keys)
    n_vis = jnp.minimum((jnp.minimum(q_pos0 + tq - 1, last_q) // CHUNK + 1) * CHUNK, n_keys)
    nk = (n_vis + tk - 1) // tk

    def score_tile(kt, carry):
        off = pl.multiple_of(kt * tk, tk)
        kidx_t = kidx_ref[0, pl.ds(off, tk), :]
        score = jnp.zeros((tk, tq), jnp.float32)
        for h in range(H_I):
            z = _dot(kidx_t, qit_ref[0, h * D_IDX:(h + 1) * D_IDX, :])
            score = score + wt_ref[0, h:h + 1, :] * jnp.maximum(z, 0.0)
        s_idx = _key_index(kt, tk)
        admissible = (s_idx // CHUNK <= q_chunk) & (s_idx < n_keys)
        return _store_scores(sc_scr, off, tk, score, admissible, carry)

    s_lo, s_hi = lax.fori_loop(0, nk, score_tile, _score_range_init(tq))
    s_lo = jnp.min(s_lo, axis=0, keepdims=True)
    s_hi = jnp.max(s_hi, axis=0, keepdims=True)
    thr = _select_threshold(sc_scr, nk, tk, n_adm, s_lo, s_hi, topk)

    acc_scr[...] = jnp.zeros(acc_scr.shape, jnp.float32)
    ones_rows = jnp.ones((ACC_ROWS - DH_A, tk), MXU_DTYPE)

    def attend_tile(kt, m_all):
        off = pl.multiple_of(kt * tk, tk)
        keep = sc_scr[pl.ds(off, tk), :] >= thr
        m_out = []
        for g in range(KV_A):
            heads = range(g * GROUP_A, (g + 1) * GROUP_A)
            k_t = k_ref[0, pl.ds(off, tk), g * DH_A:(g + 1) * DH_A]
            vt_ext = jnp.concatenate([vt_ref[0, kt, g * DH_A:(g + 1) * DH_A, :], ones_rows], axis=0)
            s = [jnp.where(keep, _dot(k_t, qat_ref[0, h * DH_A:(h + 1) * DH_A, :]), -jnp.inf) for h in heads]
            m_new = [jnp.maximum(m_all[h], jnp.max(s_h, axis=0, keepdims=True)) for h, s_h in zip(heads, s)]
            m_safe = [jnp.where(m == -jnp.inf, 0.0, m) for m in m_new]
            p = [_mx(jnp.exp(s_h - m)) for s_h, m in zip(s, m_safe)]
            pv = [_dot(vt_ext, p_h) for p_h in p]
            for h, m, pv_h in zip(heads, m_safe, pv):
                acc_scr[h] = jnp.exp(m_all[h] - m) * acc_scr[h] + pv_h
            m_out += m_new
        return tuple(m_out)

    m_init = tuple(jnp.full((1, tq), -jnp.inf, jnp.float32) for _ in range(H_A))
    lax.fori_loop(0, nk, attend_tile, m_init)
    outs = [acc_scr[h, :DH_A, :] / acc_scr[h, DH_A:DH_A + 1, :] for h in range(H_A)]
    o_ref[0] = jnp.concatenate(outs, axis=0).T.astype(o_ref.dtype)


def _dsa_attention(qat, qit, wt, k_all, vt_all, kidx_all, *, batch, past_len, n_keys, n_query):
    tq = qat.shape[2]
    nq = qat.shape[0] // batch
    lp = k_all.shape[1]
    tk = KEY_TILE
    topk = min(TOPK_MAX, n_keys // 4)
    kern = functools.partial(_dsa_kernel, tq=tq, tk=tk, past_len=past_len, n_keys=n_keys, n_query=n_query, topk=topk)
    qspec = lambda rows: pl.BlockSpec((1, rows, tq), lambda i, j: (i * nq + j, 0, 0))
    return pl.pallas_call(
        kern,
        grid=(batch, nq),
        in_specs=[
            qspec(H_A * DH_A), qspec(H_I * D_IDX), qspec(H_I),
            pl.BlockSpec((1, lp, KV_A * DH_A), lambda i, j: (i, 0, 0)),
            pl.BlockSpec((1, lp // tk, KV_A * DH_A, tk), lambda i, j: (i, 0, 0, 0)),
            pl.BlockSpec((1, lp, D_IDX), lambda i, j: (i, 0, 0)),
        ],
        out_specs=pl.BlockSpec((1, tq, H_A * DH_A), lambda i, j: (i, j, 0)),
        out_shape=jax.ShapeDtypeStruct((batch, nq * tq, H_A * DH_A), MXU_DTYPE),
        scratch_shapes=[pltpu.VMEM((lp, tq), jnp.float32), pltpu.VMEM((H_A, ACC_ROWS, tq), jnp.float32)],
        compiler_params=_params(("parallel", "arbitrary")),
        name="dsa_attention",
    )(qat, qit, wt, k_all, vt_all, kidx_all)


def _dsa_packed_kernel(qi_ref, qa_ref, w_ref, k_ref, vt_ref, kidx_ref, o_ref, sc_scr, *,
                       tk, nq, past_len, n_keys, topk):
    lanes = LANES
    lane = lax.broadcasted_iota(jnp.int32, (1, lanes), 1)
    q_chunk = (past_len + lane % nq) // CHUNK
    n_adm = jnp.minimum((q_chunk + 1) * CHUNK, n_keys)
    nk = sc_scr.shape[0] // tk

    def score_tile(kt, carry):
        off = pl.multiple_of(kt * tk, tk)
        z = _dot(kidx_ref[0, pl.ds(off, tk), :], qi_ref[0])
        part = w_ref[0] * jnp.maximum(z, 0.0)
        shift = lanes // 2
        while shift >= nq:
            part = part + pltpu.roll(part, shift, axis=1)
            shift //= 2
        s_idx = _key_index(kt, tk)
        admissible = (s_idx // CHUNK <= q_chunk) & (s_idx < n_keys)
        return _store_scores(sc_scr, off, tk, part, admissible, carry)

    s_lo, s_hi = lax.fori_loop(0, nk, score_tile, _score_range_init(lanes))
    s_lo = jnp.min(s_lo, axis=0, keepdims=True)
    s_hi = jnp.max(s_hi, axis=0, keepdims=True)
    thr = _select_threshold(sc_scr, nk, tk, n_adm, s_lo, s_hi, topk)

    ones_rows = jnp.ones((ACC_ROWS - DH_A, tk), MXU_DTYPE)

    def attend_tile(kt, carry):
        m_old, acc = carry
        off = pl.multiple_of(kt * tk, tk)
        keep = sc_scr[pl.ds(off, tk), :] >= thr
        s = jnp.where(keep, _dot(k_ref[0, pl.ds(off, tk), :], qa_ref[0]), -jnp.inf)
        m_new = jnp.maximum(m_old, jnp.max(s, axis=0, keepdims=True))
        m_safe = jnp.where(m_new == -jnp.inf, 0.0, m_new)
        p = _mx(jnp.exp(s - m_safe))
        vt_ext = jnp.concatenate([vt_ref[0, kt], ones_rows], axis=0)
        return m_new, jnp.exp(m_old - m_safe) * acc + _dot(vt_ext, p)

    init = (jnp.full((1, lanes), -jnp.inf, jnp.float32),
            jnp.zeros((KV_A * DH_A + ACC_ROWS - DH_A, lanes), jnp.float32))
    _, acc = lax.fori_loop(0, nk, attend_tile, init)
    group = lane // (GROUP_A * nq)
    out = acc[0:DH_A]
    for g in range(1, KV_A):
        out = jnp.where(group == g, acc[g * DH_A:(g + 1) * DH_A], out)
    o_ref[0] = (out / acc[KV_A * DH_A:KV_A * DH_A + 1]).astype(o_ref.dtype)


def _dsa_packed_attention(qi_p, qa_p, w_p, k_all, vt_all, kidx_all, *, nq, past_len, n_keys):
    b = qi_p.shape[0]
    lp = k_all.shape[1]
    tk = vt_all.shape[3]
    topk = min(TOPK_MAX, n_keys // 4)
    kern = functools.partial(_dsa_packed_kernel, tk=tk, nq=nq, past_len=past_len, n_keys=n_keys, topk=topk)
    per_batch = lambda *dims: pl.BlockSpec((1,) + dims, lambda i: (i,) + (0,) * len(dims))
    return pl.pallas_call(
        kern,
        grid=(b,),
        in_specs=[per_batch(D_IDX, LANES), per_batch(KV_A * DH_A, LANES), per_batch(1, LANES),
                  per_batch(lp, KV_A * DH_A), per_batch(lp // tk, KV_A * DH_A, tk), per_batch(lp, D_IDX)],
        out_specs=per_batch(DH_A, LANES),
        out_shape=jax.ShapeDtypeStruct((b, DH_A, LANES), MXU_DTYPE),
        scratch_shapes=[pltpu.VMEM((lp, LANES), jnp.float32)],
        compiler_params=_params(("parallel",)),
        name="dsa_packed_attention",
    )(qi_p, qa_p, w_p, k_all, vt_all, kidx_all)


def _softplus(x):
    return jnp.maximum(x, 0.0) + jnp.log(1.0 + jnp.exp(-jnp.abs(x)))


def _unit_lower_inverses(a_strict, c):
    eye = (lax.broadcasted_iota(jnp.int32, (c, c), 0) == lax.broadcasted_iota(jnp.int32, (c, c), 1)
           ).astype(jnp.float32)
    power = [-a for a in a_strict]
    inv = [eye + p for p in power]
    span = 2
    while span < c:
        pb = [_mx(p) for p in power]
        power = [_dot(p, p) for p in pb]
        inv = [i + _dot(_mx(i), _mx(p)) for i, p in zip(inv, power)]
        span *= 2
    return inv


def _gdn_kernel(qkvb_ref, gateb_ref, misc_ref, a_t_ref, conv0_ref, ssm0_ref, convw_ref,
                alog_r_ref, dtb_r_ref, alog_c_ref, dtb_c_ref, normw_ref,
                y_ref, ssm_ref, tail_scr, *, c, n_chunks):
    rows = c * n_chunks

    @pl.when(pl.program_id(1) == 0)
    def _():
        tail_scr[...] = conv0_ref[0]
        ssm_ref[...] = ssm0_ref[...]

    x = qkvb_ref[0]
    ext = jnp.concatenate([tail_scr[...], x], axis=0)
    conv = None
    for j in range(CONV_W):
        shifted = ext if j == CONV_W - 1 else pltpu.roll(ext, CONV_W - 1 - j, axis=0)
        term = shifted[SUBLANES:SUBLANES + rows] * convw_ref[j:j + 1, :]
        conv = term if conv is None else conv + term
    tail_scr[...] = x[rows - SUBLANES:rows]
    act = conv * _sigmoid(conv)

    misc = misc_ref[0]
    beta_all = _sigmoid(misc)
    g_col = -jnp.exp(alog_r_ref[...]) * _softplus(misc + dtb_r_ref[...])
    ri = lax.broadcasted_iota(jnp.int32, (rows, rows), 0)
    ci = lax.broadcasted_iota(jnp.int32, (rows, rows), 1)
    same_chunk_causal = ((ci <= ri) & (ci // c == ri // c)).astype(jnp.float32)
    big_g_col = _exact_dot(same_chunk_causal, g_col)
    upper = (lax.broadcasted_iota(jnp.int32, (LANES, LANES), 0)
             <= lax.broadcasted_iota(jnp.int32, (LANES, LANES), 1)).astype(jnp.float32)
    rc = lax.broadcasted_iota(jnp.int32, (c, c), 0)
    cc = lax.broadcasted_iota(jnp.int32, (c, c), 1)
    causal = cc <= rc
    strict = cc < rc

    big_g_row = [_exact_dot(-jnp.exp(alog_c_ref[...]) * _softplus(a_t_ref[0, n] + dtb_c_ref[...]), upper)
                 for n in range(n_chunks)]
    pairs = [(n, h) for n in range(n_chunks) for h in range(H_B)]
    tile = lambda a, n, col, width: a[n * c:(n + 1) * c, col:col + width]
    l2n = lambda a: a * lax.rsqrt(jnp.sum(a * a, axis=1, keepdims=True) + NORM_EPS)
    q = [l2n(tile(act, n, h * DK_B, DK_B)) * (DK_B ** -0.5) for n, h in pairs]
    k = [l2n(tile(act, n, (H_B + h) * DK_B, DK_B)) for n, h in pairs]
    v = [tile(act, n, 2 * H_B * DK_B + h * DV_B, DV_B) for n, h in pairs]
    gc = [tile(big_g_col, n, MISC_A + h, 1) for n, h in pairs]
    gr = [big_g_row[n][h:h + 1, :c] for n, h in pairs]
    beta = [tile(beta_all, n, MISC_B + h, 1) for n, h in pairs]
    g_last = [g[c - 1:c, :] for g in gc]
    exp_g = [jnp.exp(g) for g in gc]
    decay = [jnp.where(causal, jnp.exp(jnp.where(causal, a - b, 0.0)), 0.0) for a, b in zip(gc, gr)]
    kb = [_mx(a) for a in k]
    kk = [_nt_dot(a, a) for a in kb]
    qk = [_nt_dot(_mx(a), b) for a, b in zip(q, kb)]
    a_mat = [jnp.where(strict, b * m * d, 0.0) for b, m, d in zip(beta, kk, decay)]
    inv = _unit_lower_inverses(a_mat, c)
    rhs = [_mx(jnp.concatenate([vv * b, kx * (b * e)], axis=1)) for vv, kx, b, e in zip(v, k, beta, exp_g)]
    sol = [_dot(_mx(i), r) for i, r in zip(inv, rhs)]
    u = [s[:, :DV_B] for s in sol]
    w_qd = [_mx(jnp.concatenate([s[:, DV_B:], a * e], axis=0)) for s, a, e in zip(sol, q, exp_g)]
    qk = [_mx(m * d) for m, d in zip(qk, decay)]
    k_dec = [_mx(a * jnp.exp(gl - g)) for a, gl, g in zip(k, g_last, gc)]
    g_tot = [jnp.exp(gl) for gl in g_last]

    for n in range(n_chunks):
        ids = [n * H_B + h for h in range(H_B)]
        state = [ssm_ref[0, h] for h in range(H_B)]
        both = [_dot(w_qd[i], _mx(s)) for i, s in zip(ids, state)]
        v_new = [_mx(u[i] - b[:c]) for i, b in zip(ids, both)]
        delta = [_tn_dot(k_dec[i], vn) for i, vn in zip(ids, v_new)]
        for h, i in enumerate(ids):
            ssm_ref[0, h] = state[h] * g_tot[i] + delta[h]
        o = [b[c:] + _dot(qk[i], vn) for i, b, vn in zip(ids, both, v_new)]
        for h in range(H_B):
            oh = o[h] * lax.rsqrt(jnp.mean(o[h] * o[h], axis=1, keepdims=True) + NORM_EPS) * normw_ref[...]
            gate = gateb_ref[0, n * c:(n + 1) * c, h * DV_B:(h + 1) * DV_B]
            y_ref[0, n * c:(n + 1) * c, h * DV_B:(h + 1) * DV_B] = (oh * (gate * _sigmoid(gate))).astype(y_ref.dtype)


def _gated_delta(qkvb, gateb, misc, a_t, conv0, ssm0, conv_w, a_log, dt_bias, norm_w):
    b, t, _ = qkvb.shape
    c = min(CHUNK, t)
    n_chunks = max(1, min(4, t // c))
    rows = c * n_chunks
    in_slab = lambda v: jnp.pad(v, (MISC_A, MISC_W_COLS - MISC_A - H_B)).reshape(1, MISC_W_COLS)
    in_rows = lambda v: jnp.pad(v, (0, SUBLANES - H_B)).reshape(SUBLANES, 1)
    small = [in_slab(a_log), in_slab(dt_bias), in_rows(a_log), in_rows(dt_bias), norm_w.reshape(1, DV_B)]
    return pl.pallas_call(
        functools.partial(_gdn_kernel, c=c, n_chunks=n_chunks),
        grid=(b, t // rows),
        in_specs=[
            pl.BlockSpec((1, rows, CONV_DIM), lambda i, j: (i, j, 0)),
            pl.BlockSpec((1, rows, H_B * DV_B), lambda i, j: (i, j, 0)),
            pl.BlockSpec((1, rows, MISC_W_COLS), lambda i, j: (i, j, 0)),
            pl.BlockSpec((1, n_chunks, SUBLANES, LANES), lambda i, j: (i, j, 0, 0)),
            pl.BlockSpec((1, SUBLANES, CONV_DIM), lambda i, j: (i, 0, 0)),
            pl.BlockSpec((1, H_B, DK_B, DV_B), lambda i, j: (i, 0, 0, 0)),
            _const_spec(conv_w.shape),
        ] + [_const_spec(s.shape) for s in small],
        out_specs=[
            pl.BlockSpec((1, rows, H_B * DV_B), lambda i, j: (i, j, 0)),
            pl.BlockSpec((1, H_B, DK_B, DV_B), lambda i, j: (i, 0, 0, 0)),
        ],
        out_shape=[jax.ShapeDtypeStruct((b, t, H_B * DV_B), MXU_DTYPE),
                   jax.ShapeDtypeStruct((b, H_B, DK_B, DV_B), jnp.float32)],
        scratch_shapes=[pltpu.VMEM((SUBLANES, CONV_DIM), jnp.float32)],
        compiler_params=_params(("parallel", "arbitrary")),
        name="gated_delta",
    )(qkvb, gateb, misc, a_t, conv0, ssm0, conv_w, *small)


def _layer_norm(x, g, b):
    mu = jnp.mean(x, axis=-1, keepdims=True)
    xc = x - mu
    var = jnp.mean(xc * xc, axis=-1, keepdims=True)
    return xc * lax.rsqrt(var + LN_EPS) * g + b


def _merge_kernel(x_ref, gates_ref, ya_ref, yb_ref, bgate_ref, wpa_ref, wpb_ref, wout_ref, g_ref, b_ref, o_ref):
    gates = _sigmoid(gates_ref[...] + bgate_ref[...])
    mixed = (gates[:, :D_MODEL] * _dot(ya_ref[...], wpa_ref[...])
             + gates[:, D_MODEL:] * _dot(yb_ref[...], wpb_ref[...]))
    mix_out = _dot(_mx(mixed), wout_ref[...])
    o_ref[...] = _layer_norm(DEEPNORM_ALPHA * x_ref[...] + mix_out, g_ref[...], b_ref[...])


def _merge(x2d, gates, ya, yb, b_gate, wpa, wpb, wout, ln_g, ln_b):
    n = x2d.shape[0]
    tm = TOKEN_TILE
    row = lambda w: pl.BlockSpec((tm, w), lambda i: (i, 0))
    consts = [b_gate.reshape(1, -1), wpa, wpb, wout, ln_g.reshape(1, -1), ln_b.reshape(1, -1)]
    return pl.pallas_call(
        _merge_kernel,
        grid=(n // tm,),
        in_specs=[row(D_MODEL), row(2 * D_MODEL), row(H_A * DH_A), row(H_B * DV_B)]
        + [_const_spec(c.shape) for c in consts],
        out_specs=row(D_MODEL),
        out_shape=jax.ShapeDtypeStruct((n, D_MODEL), jnp.float32),
        compiler_params=_params(("parallel",)),
        name="merge_out_ln",
    )(x2d, gates, ya, yb, *consts)


def _ffn_kernel(x_ref, wup_ref, wdown_ref, g_ref, b_ref, o_ref):
    x = x_ref[...]
    xb = _mx(x)
    acc = jnp.zeros(x.shape, jnp.float32)
    for c0 in range(0, D_FF, F_CHUNK):
        f_gate = _dot(xb, wup_ref[:, c0:c0 + F_CHUNK])
        f_up = _dot(xb, wup_ref[:, D_FF + c0:D_FF + c0 + F_CHUNK])
        hidden = f_gate * _sigmoid(f_gate) * f_up
        acc = acc + _dot(_mx(hidden), wdown_ref[c0:c0 + F_CHUNK, :])
    o_ref[...] = _layer_norm(DEEPNORM_ALPHA * x + acc, g_ref[...], b_ref[...])


def _ffn(x2d, w_up, w_down, ln_g, ln_b):
    n = x2d.shape[0]
    tm = TOKEN_TILE
    row = pl.BlockSpec((tm, D_MODEL), lambda i: (i, 0))
    consts = [w_up, w_down, ln_g.reshape(1, -1), ln_b.reshape(1, -1)]
    return pl.pallas_call(
        _ffn_kernel,
        grid=(n // tm,),
        in_specs=[row] + [_const_spec(c.shape) for c in consts],
        out_specs=row,
        out_shape=jax.ShapeDtypeStruct((n, D_MODEL), jnp.float32),
        compiler_params=_params(("parallel",)),
        name="swiglu_ln",
    )(x2d, *consts)


def _round_up(a, m):
    return (a + m - 1) // m * m


def _trunk_layer(x, past_k, past_v, past_kidx, conv_state, ssm_state, prm):
    (w_rows, w_cols, b_gate, conv_w, a_log, dt_bias, norm_w, wpa, wpb, wout, ln1_g, ln1_b, w_up, w_down, ln2_g,
     ln2_b) = prm
    b, t, _ = x.shape
    n = b * t
    assert n % TOKEN_TILE == 0
    past_len = past_k.shape[1]
    n_keys = past_len + t
    x2d = x.reshape(n, D_MODEL)
    ka, va, misc, qkvb, gateb, gates, kb, kidxb, qat, qit, vt, wt = _in_proj(x2d, w_rows, w_cols)

    k_new = ka.reshape(b, t, KV_A, DH_A)
    v_new = va.reshape(b, t, KV_A, DH_A)
    kidx_new = misc[:, MISC_KIDX:MISC_KIDX + D_IDX].reshape(b, t, D_IDX)
    if past_len == 0 and t % TOKEN_TILE == 0:
        k_all = kb.reshape(b, t, KV_A * DH_A)
        kidx_all = kidxb.reshape(b, t, D_IDX)
        vt_all = vt.reshape(b, t // KEY_TILE, KV_A * DH_A, KEY_TILE)
        y_a = _dsa_attention(qat, qit, wt, k_all, vt_all, kidx_all, batch=b, past_len=past_len, n_keys=n_keys,
                             n_query=t)
        y_a = y_a.reshape(n, H_A * DH_A)
    else:
        assert H_A * t == LANES and H_I == H_A and TOKEN_TILE % t == 0
        per_tile = TOKEN_TILE // t
        lp = _round_up(n_keys, PACKED_KEY_TILE)
        pad_keys = lambda a: jnp.pad(a, ((0, 0), (0, lp - n_keys), (0, 0)))
        k_all = pad_keys(jnp.concatenate([_mx(past_k).reshape(b, past_len, -1), kb.reshape(b, t, -1)], axis=1))
        kidx_all = pad_keys(jnp.concatenate([_mx(past_kidx), kidxb.reshape(b, t, -1)], axis=1))
        v_all = pad_keys(jnp.concatenate([_mx(past_v).reshape(b, past_len, -1), _mx(va).reshape(b, t, -1)], axis=1))
        vt_all = jnp.transpose(v_all.reshape(b, lp // PACKED_KEY_TILE, PACKED_KEY_TILE, KV_A * DH_A), (0, 1, 3, 2))

        def heads_on_lanes(a, d):
            a = a.reshape(-1, H_A, d, per_tile, t)
            return jnp.transpose(a, (0, 3, 2, 1, 4)).reshape(b, d, H_A, t)

        qi_p = heads_on_lanes(qit, D_IDX).reshape(b, D_IDX, LANES)
        qa_hd = heads_on_lanes(qat, DH_A)
        head_group = (jnp.arange(H_A) // GROUP_A)[None, None, :, None]
        qa_p = jnp.concatenate([jnp.where(head_group == g, qa_hd, 0) for g in range(KV_A)], axis=1)
        w_p = heads_on_lanes(wt, 1).reshape(b, 1, LANES)
        y_t = _dsa_packed_attention(qi_p, qa_p.reshape(b, KV_A * DH_A, LANES), w_p, k_all, vt_all, kidx_all,
                                    nq=t, past_len=past_len, n_keys=n_keys)
        y_a = jnp.transpose(y_t.reshape(b, DH_A, H_A, t), (0, 3, 2, 1)).reshape(n, H_A * DH_A)

    c = min(CHUNK, t)
    a_t = misc[:, MISC_A:MISC_A + SUBLANES].reshape(b, t // c, c, SUBLANES)
    a_t = jnp.pad(jnp.transpose(a_t, (0, 1, 3, 2)), ((0, 0), (0, 0), (0, 0), (0, LANES - c)))
    qkvb3 = qkvb.reshape(b, t, CONV_DIM)
    conv0 = jnp.pad(conv_state, ((0, 0), (SUBLANES - (CONV_W - 1), 0), (0, 0)))
    y_b, new_ssm = _gated_delta(qkvb3, gateb.reshape(b, t, -1), misc.reshape(b, t, -1), a_t, conv0,
                                ssm_state, conv_w, a_log, dt_bias, norm_w)
    new_conv = jnp.concatenate([conv_state, qkvb3], axis=1)[:, t:]

    x1 = _merge(x2d, gates, y_a, y_b.reshape(n, -1), b_gate, wpa, wpb, wout, ln1_g, ln1_b)
    x2 = _ffn(x1, w_up, w_down, ln2_g, ln2_b)
    return x2.reshape(b, t, D_MODEL), k_new, v_new, kidx_new, new_conv, new_ssm


def _layer_params(l, w_in, b_gate, conv_w, a_log, dt_bias, gdn_norm_w, w_proj_a, w_proj_b, w_out, ln1_g, ln1_b,
                  w_up, w_down, ln2_g, ln2_b):
    w_rows, w_cols = _arrange_w_in(w_in[l])
    return (_mx(w_rows), _mx(w_cols), b_gate[l], conv_w[l], a_log[l], dt_bias[l], gdn_norm_w[l],
            _mx(w_proj_a[l]), _mx(w_proj_b[l]), _mx(w_out[l]), ln1_g[l], ln1_b[l],
            _mx(w_up[l]), _mx(w_down[l]), ln2_g[l], ln2_b[l])


def kernel(x_prompt, x_sample, cache_k, cache_v, cache_kidx, state_conv, state_ssm, w_in, b_gate, conv_w, a_log,
           dt_bias, gdn_norm_w, w_proj_a, w_proj_b, w_out, ln1_g, ln1_b, w_up, w_down, ln2_g, ln2_b):
    yp, ys = x_prompt, x_sample
    bp = x_prompt.shape[0]
    dt = x_prompt.dtype
    st_p, st_s = [], []
    for l in range(DEPTH):
        prm = _layer_params(l, w_in, b_gate, conv_w, a_log, dt_bias, gdn_norm_w, w_proj_a, w_proj_b, w_out,
                            ln1_g, ln1_b, w_up, w_down, ln2_g, ln2_b)
        empty_kv = jnp.zeros((bp, 0, KV_A, DH_A), dt)
        yp, *sp = _trunk_layer(yp, empty_kv, empty_kv, jnp.zeros((bp, 0, D_IDX), dt),
                               jnp.zeros((bp, CONV_W - 1, CONV_DIM), dt),
                               jnp.zeros((bp, H_B, DK_B, DV_B), dt), prm)
        ys, *ss = _trunk_layer(ys, cache_k[l], cache_v[l], cache_kidx[l], state_conv[l], state_ssm[l], prm)
        st_p.append(sp)
        st_s.append(ss)
    stack = lambda states, i: jnp.stack([s[i] for s in states], axis=0)
    return (yp, ys) + tuple(stack(st_p, i) for i in range(5)) + tuple(stack(st_s, i) for i in range(5))
```

```python
import functools
import math

import numpy as np
import jax
import jax.numpy as jnp
from jax import lax
from jax.experimental import pallas as pl
from jax.experimental.pallas import tpu as pltpu

D_MODEL = 1024
DEPTH = 2
CHUNK = 64
H_A = 8
KV_A = 2
DH_A = 64
GROUP_A = H_A // KV_A
H_I = 8
D_IDX = 64
TOPK_MAX = 256
INDEX_SCALE = (H_I * D_IDX) ** -0.5
H_B = 4
DK_B = 128
DV_B = 128
CONV_W = 4
CONV_DIM = 2 * H_B * DK_B + H_B * DV_B
D_FF = int(math.ceil(8 * D_MODEL / 3 / 256)) * 256
DEEPNORM_ALPHA = (2 * DEPTH) ** 0.25
LN_EPS = 1e-5
NORM_EPS = 1e-6
IN_SIZES = (H_A * DH_A, KV_A * DH_A, KV_A * DH_A, H_I * D_IDX, D_IDX, H_I, CONV_DIM, H_B, H_B,
            H_B * DV_B, 2 * D_MODEL)

MXU_DTYPE = jnp.bfloat16
VMEM_LIMIT_BYTES = 56 * 1024 * 1024
LANES = 128
SUBLANES = 8
FLT_MAX = float(np.finfo(np.float32).max)

TOKEN_TILE = 256
FFN_TILE = 512
KEY_TILE = 256
PACKED_KEY_TILE = 512
ATTN_GROUP = 2
F_CHUNK = 256

MISC_KIDX = 0
MISC_A = D_IDX + H_I
MISC_B = MISC_A + H_B
MISC_W_COLS = LANES

T_QA = 0
T_QI = H_A * DH_A
T_V = T_QI + H_I * D_IDX
T_W = T_V + KV_A * DH_A
T_ROWS = T_W + 16
ACC_ROWS = DH_A + 16


def _dot(a, b):
    return jnp.dot(a, b, preferred_element_type=jnp.float32)


def _nt_dot(a, b):
    return lax.dot_general(a, b, (((1,), (1,)), ((), ())), preferred_element_type=jnp.float32)


def _tn_dot(a, b):
    return lax.dot_general(a, b, (((0,), (0,)), ((), ())), preferred_element_type=jnp.float32)


def _exact_dot(a, b):
    return jnp.dot(a, b, preferred_element_type=jnp.float32, precision=lax.Precision.HIGHEST)


def _mx(a):
    return a.astype(MXU_DTYPE)


def _sigmoid(x):
    return 1.0 / (1.0 + jnp.exp(-x))


def _params(sem):
    return pltpu.CompilerParams(dimension_semantics=sem, vmem_limit_bytes=VMEM_LIMIT_BYTES)


def _const_spec(shape):
    zeros = (0,) * len(shape)
    return pl.BlockSpec(shape, lambda *_: zeros)


_ROW_GROUPS = (
    ("ka", KV_A * DH_A, jnp.float32),
    ("va", KV_A * DH_A, jnp.float32),
    ("misc", MISC_W_COLS, jnp.float32),
    ("qkvb", CONV_DIM, jnp.float32),
    ("gateb", H_B * DV_B, jnp.float32),
    ("gates", 2 * D_MODEL, jnp.float32),
)


def _arrange_w_in(w_in):
    offs = np.cumsum((0,) + IN_SIZES)
    col = lambda i: w_in[:, offs[i]:offs[i + 1]]
    misc = jnp.concatenate([col(4), col(5), col(7), col(8)], axis=1)
    misc = jnp.pad(misc, ((0, 0), (0, MISC_W_COLS - misc.shape[1])))
    w_rows = jnp.concatenate([col(1), col(2), misc, col(6), col(9), col(10)], axis=1)
    w_cols = jnp.concatenate([col(0), col(3), col(2), col(5)], axis=1).T
    w_cols = jnp.pad(w_cols, ((0, T_ROWS - w_cols.shape[0]), (0, 0)))
    return w_rows, w_cols


def _in_proj_kernel(x_ref, w_ref, wt_ref, ka_ref, va_ref, misc_ref, qkvb_ref, gateb_ref, gates_ref,
                    kb_ref, kidxb_ref, qat_ref, qit_ref, vt_ref, wt_out_ref):
    xb = _mx(x_ref[...])
    off = 0
    for (_, width, _), o_ref in zip(_ROW_GROUPS, (ka_ref, va_ref, misc_ref, qkvb_ref, gateb_ref, gates_ref)):
        for c0 in range(0, width, 512):
            c1 = min(c0 + 512, width)
            o_ref[:, c0:c1] = _dot(xb, w_ref[:, off + c0:off + c1])
        off += width
    kb_ref[...] = _mx(ka_ref[...])
    kidxb_ref[...] = _mx(misc_ref[:, MISC_KIDX:MISC_KIDX + D_IDX])
    qat_ref[0] = _mx(_nt_dot(wt_ref[T_QA:T_QI, :], xb) * (DH_A ** -0.5))
    qit_ref[0] = _mx(_nt_dot(wt_ref[T_QI:T_V, :], xb))
    vt_ref[0] = _mx(_nt_dot(wt_ref[T_V:T_W, :], xb))
    wt_out_ref[0] = _nt_dot(wt_ref[T_W:T_ROWS, :], xb)[:H_I] * INDEX_SCALE


def _in_proj(x2d, w_rows, w_cols):
    n = x2d.shape[0]
    tm = TOKEN_TILE
    nb = n // tm
    row = lambda w: pl.BlockSpec((tm, w), lambda i: (i, 0))
    col = lambda r: pl.BlockSpec((1, r, tm), lambda i: (i, 0, 0))
    out_specs = [row(w) for _, w, _ in _ROW_GROUPS] + [row(KV_A * DH_A), row(D_IDX)] + [
        col(H_A * DH_A), col(H_I * D_IDX), col(KV_A * DH_A), col(H_I)]
    out_shape = [jax.ShapeDtypeStruct((n, w), dt) for _, w, dt in _ROW_GROUPS] + [
        jax.ShapeDtypeStruct((n, KV_A * DH_A), MXU_DTYPE), jax.ShapeDtypeStruct((n, D_IDX), MXU_DTYPE),
        jax.ShapeDtypeStruct((nb, H_A * DH_A, tm), MXU_DTYPE), jax.ShapeDtypeStruct((nb, H_I * D_IDX, tm), MXU_DTYPE),
        jax.ShapeDtypeStruct((nb, KV_A * DH_A, tm), MXU_DTYPE), jax.ShapeDtypeStruct((nb, H_I, tm), jnp.float32)]
    return pl.pallas_call(
        _in_proj_kernel,
        grid=(nb,),
        in_specs=[row(D_MODEL), _const_spec(w_rows.shape), _const_spec(w_cols.shape)],
        out_specs=out_specs,
        out_shape=out_shape,
        compiler_params=_params(("parallel",)),
        name="in_proj",
    )(x2d, w_rows, w_cols)


def _order_key(x):
    bits = lax.bitcast_convert_type(x, jnp.int32)
    return bits ^ ((bits >> 31) & jnp.int32(0x7FFFFFFF))


def _from_order_key(key):
    return lax.bitcast_convert_type(key ^ ((key >> 31) & jnp.int32(0x7FFFFFFF)), jnp.float32)


_MAX_SUBNORMAL_KEY = 0x007FFFFF


def _order_rank(x):
    key = _order_key(x)
    return jnp.where(key > _MAX_SUBNORMAL_KEY, key - _MAX_SUBNORMAL_KEY,
                     jnp.where(key < -_MAX_SUBNORMAL_KEY - 1, key + _MAX_SUBNORMAL_KEY + 1, 0))


def _from_order_rank(rank):
    key = jnp.where(rank > 0, rank + _MAX_SUBNORMAL_KEY, jnp.where(rank < 0, rank - _MAX_SUBNORMAL_KEY - 1, 0))
    return _from_order_key(key)


FOLD_ROWS = 4 * SUBLANES


def _fold(x, op):
    return op(x.reshape(x.shape[0] // FOLD_ROWS, FOLD_ROWS, x.shape[1]), axis=0)


def _key_index(kt, tk):
    return kt * tk + lax.broadcasted_iota(jnp.int32, (tk, 1), 0)


def _select_threshold(sc_scr, nk, tk, n_adm, s_lo, s_hi, topk):
    lanes = sc_scr.shape[1]
    kf = float(topk)

    def count(pred):
        def body(kt, acc):
            off = pl.multiple_of(kt * tk, tk)
            hit = jnp.where(pred(sc_scr[pl.ds(off, tk), :], _key_index(kt, tk)), 1.0, 0.0)
            return acc + _fold(hit, jnp.sum)
        acc = lax.fori_loop(0, nk, body, jnp.zeros((FOLD_ROWS, lanes), jnp.float32))
        return jnp.sum(acc, axis=0, keepdims=True)

    n_adm_f = n_adm.astype(jnp.float32)
    take_all = n_adm_f <= kf

    def search_step(st):
        it, lo_key, hi_key, c_lo, c_hi, done = st
        lo_f = _from_order_rank(lo_key)
        hi_f = _from_order_rank(hi_key)
        interp = lo_f + (hi_f - lo_f) * ((c_lo - kf) / (c_lo - c_hi))
        middle = 0.5 * lo_f + 0.5 * hi_f
        by_value = _order_rank(jnp.where(it % 2 == 0, interp, middle))
        by_key = (lo_key >> 1) + (hi_key >> 1) + ((lo_key | hi_key) & 1)
        cand_key = jnp.where(it < 2, it, jnp.where(it < 40, by_value, by_key))
        cand_key = jnp.minimum(jnp.maximum(cand_key, lo_key + 1), hi_key)
        cand = _from_order_rank(cand_key)
        c = count(lambda x, _: x >= cand)
        active = done < 0.5
        up = active & (c >= kf)
        down = active & (c < kf)
        lo_key = jnp.where(up, cand_key, lo_key)
        c_lo = jnp.where(up, c, c_lo)
        hi_key = jnp.where(down, cand_key - 1, hi_key)
        c_hi = jnp.where(down, c, c_hi)
        done = jnp.where((c_lo == kf) | (lo_key >= hi_key), 1.0, done)
        return it + 1, lo_key, hi_key, c_lo, c_hi, done

    def search_cond(st):
        return (st[0] < 80) & (jnp.min(st[5]) < 0.5)

    lo_key0 = _order_rank(s_lo)
    hi_key0 = _order_rank(s_hi)
    done0 = jnp.where(take_all | (lo_key0 >= hi_key0), 1.0, 0.0)
    st = (jnp.int32(0), lo_key0, hi_key0, n_adm_f, jnp.zeros((1, lanes), jnp.float32), done0)
    _, lo_key, _, c_lo, c_hi, _ = lax.while_loop(search_cond, lambda s: search_step(search_step(s)), st)
    thr = jnp.where(take_all, -FLT_MAX, _from_order_rank(lo_key))

    excess = (c_lo > kf) & jnp.logical_not(take_all)
    n_rows = sc_scr.shape[0]

    @pl.when(jnp.max(jnp.where(excess, 1.0, 0.0)) > 0.5)
    def _():
        need = kf - c_hi
        n_bits = max(1, int(n_rows - 1).bit_length())

        def idx_step(i, bound):
            cand = bound | lax.shift_left(jnp.int32(1), n_bits - 1 - i)
            below = count(lambda x, idx: (x == thr) & (idx < cand))
            return jnp.where(below < need, cand, bound)

        bound = lax.fori_loop(0, n_bits, idx_step, jnp.zeros((1, lanes), jnp.int32))
        bound = jnp.where(excess, bound, jnp.int32(n_rows))

        def drop_tile(kt, carry):
            off = pl.multiple_of(kt * tk, tk)
            x = sc_scr[pl.ds(off, tk), :]
            sc_scr[pl.ds(off, tk), :] = jnp.where((x == thr) & (_key_index(kt, tk) > bound), -jnp.inf, x)
            return carry

        lax.fori_loop(0, nk, drop_tile, 0)

    return thr


def _store_scores(sc_scr, off, tk, score, admissible, carry):
    lo, hi = carry
    if admissible is None:
        sc_scr[pl.ds(off, tk), :] = score
        return jnp.minimum(lo, _fold(score, jnp.min)), jnp.maximum(hi, _fold(score, jnp.max))
    masked = jnp.where(admissible, score, -jnp.inf)
    sc_scr[pl.ds(off, tk), :] = masked
    lo = jnp.minimum(lo, _fold(jnp.where(admissible, score, jnp.inf), jnp.min))
    hi = jnp.maximum(hi, _fold(masked, jnp.max))
    return lo, hi


def _score_range_init(lanes):
    return (jnp.full((FOLD_ROWS, lanes), jnp.inf, jnp.float32), jnp.full((FOLD_ROWS, lanes), -jnp.inf, jnp.float32))


def _dsa_kernel(qat_ref, qit_ref, wt_ref, k_ref, vt_ref, kidx_ref, o_ref, sc_scr, acc_scr, *,
                tq, tk, past_len, n_keys, n_query, topk):
    iq = pl.program_id(1)
    last_q = past_len + n_query - 1
    q_pos0 = past_len + iq * tq
    q_pos = jnp.minimum(q_pos0 + lax.broadcasted_iota(jnp.int32, (1, tq), 1), last_q)
    q_chunk = q_pos // CHUNK
    n_adm = jnp.minimum((q_chunk + 1) * CHUNK, n_keys)
    n_vis = jnp.minimum((jnp.minimum(q_pos0 + tq - 1, last_q) // CHUNK + 1) * CHUNK, n_keys)
    nk = (n_vis + tk - 1) // tk

    def score_tile(kt, carry):
        off = pl.multiple_of(kt * tk, tk)
        kidx_t = kidx_ref[0, pl.ds(off, tk), :]
        score = jnp.zeros((tk, tq), jnp.float32)
        for h in range(H_I):
            z = _dot(kidx_t, qit_ref[0, h * D_IDX:(h + 1) * D_IDX, :])
            score = score + wt_ref[0, h:h + 1, :] * jnp.maximum(z, 0.0)

        def with_mask(c):
            s_idx = _key_index(kt, tk)
            admissible = (s_idx // CHUNK <= q_chunk) & (s_idx < n_keys)
            return _store_scores(sc_scr, off, tk, score, admissible, c)

        interior = ((off + tk - 1) // CHUNK <= q_pos0 // CHUNK) & (off + tk <= n_keys)
        return lax.cond(interior, lambda c: _store_scores(sc_scr, off, tk, score, None, c), with_mask, carry)

    s_lo, s_hi = lax.fori_loop(0, nk, score_tile, _score_range_init(tq))
    s_lo = jnp.min(s_lo, axis=0, keepdims=True)
    s_hi = jnp.max(s_hi, axis=0, keepdims=True)
    thr = _select_threshold(sc_scr, nk, tk, n_adm, s_lo, s_hi, topk)

    acc_scr[...] = jnp.zeros(acc_scr.shape, jnp.float32)
    ones_rows = jnp.ones((ACC_ROWS - DH_A, tk), MXU_DTYPE)

    def masked_scores(kt):
        off = pl.multiple_of(kt * tk, tk)
        bias = jnp.where(sc_scr[pl.ds(off, tk), :] >= thr, 0.0, -jnp.inf)
        k_t = [k_ref[0, pl.ds(off, tk), g * DH_A:(g + 1) * DH_A] for g in range(KV_A)]
        return [_mx(_dot(k_t[h // GROUP_A], qat_ref[0, h * DH_A:(h + 1) * DH_A, :]) + bias) for h in range(H_A)]

    def softmax_pv(kt, s, m_all):
        m_out = []
        for g in range(KV_A):
            heads = range(g * GROUP_A, (g + 1) * GROUP_A)
            vt_ext = jnp.concatenate([vt_ref[0, kt, g * DH_A:(g + 1) * DH_A, :], ones_rows], axis=0)
            m_new = [jnp.maximum(m_all[h], jnp.max(s[h], axis=0, keepdims=True).astype(jnp.float32))
                     for h in heads]
            m_safe = [jnp.where(m == -jnp.inf, 0.0, m) for m in m_new]
            p = [jnp.exp(s[h] - _mx(m)) for h, m in zip(heads, m_safe)]
            pv = [_dot(vt_ext, p_h) for p_h in p]
            for h, m, pv_h in zip(heads, m_safe, pv):
                acc_scr[h] = jnp.exp(m_all[h] - m) * acc_scr[h] + pv_h
            m_out += m_new
        return tuple(m_out)

    def attend_group(i, m_all):
        tiles = [ATTN_GROUP * i + j for j in range(ATTN_GROUP)]
        scores = [masked_scores(kt) for kt in tiles]
        for kt, s in zip(tiles, scores):
            m_all = softmax_pv(kt, s, m_all)
        return m_all

    m_init = tuple(jnp.full((1, tq), -jnp.inf, jnp.float32) for _ in range(H_A))
    m_all = lax.fori_loop(0, nk // ATTN_GROUP, attend_group, m_init)
    lax.fori_loop(nk // ATTN_GROUP * ATTN_GROUP, nk, lambda kt, m: softmax_pv(kt, masked_scores(kt), m), m_all)

    outs = [acc_scr[h, :DH_A, :] / acc_scr[h, DH_A:DH_A + 1, :] for h in range(H_A)]
    o_ref[0] = jnp.concatenate(outs, axis=0).T.astype(o_ref.dtype)


def _dsa_attention(qat, qit, wt, k_all, vt_all, kidx_all, *, batch, past_len, n_keys, n_query):
    tq = qat.shape[2]
    nq = qat.shape[0] // batch
    lp = k_all.shape[1]
    tk = KEY_TILE
    topk = min(TOPK_MAX, n_keys // 4)
    kern = functools.partial(_dsa_kernel, tq=tq, tk=tk, past_len=past_len, n_keys=n_keys, n_query=n_query, topk=topk)
    qspec = lambda rows: pl.BlockSpec((1, rows, tq), lambda i, j: (i * nq + j, 0, 0))
    return pl.pallas_call(
        kern,
        grid=(batch, nq),
        in_specs=[
            qspec(H_A * DH_A), qspec(H_I * D_IDX), qspec(H_I),
            pl.BlockSpec((1, lp, KV_A * DH_A), lambda i, j: (i, 0, 0)),
            pl.BlockSpec((1, lp // tk, KV_A * DH_A, tk), lambda i, j: (i, 0, 0, 0)),
            pl.BlockSpec((1, lp, D_IDX), lambda i, j: (i, 0, 0)),
        ],
        out_specs=pl.BlockSpec((1, tq, H_A * DH_A), lambda i, j: (i, j, 0)),
        out_shape=jax.ShapeDtypeStruct((batch, nq * tq, H_A * DH_A), MXU_DTYPE),
        scratch_shapes=[pltpu.VMEM((lp, tq), jnp.float32), pltpu.VMEM((H_A, ACC_ROWS, tq), jnp.float32)],
        compiler_params=_params(("parallel", "arbitrary")),
        name="dsa_attention",
    )(qat, qit, wt, k_all, vt_all, kidx_all)


def _dsa_packed_kernel(qi_ref, qa_ref, w_ref, k_ref, vt_ref, kidx_ref, o_ref, sc_scr, *,
                       tk, nq, past_len, n_keys, topk):
    lanes = LANES
    lane = lax.broadcasted_iota(jnp.int32, (1, lanes), 1)
    q_chunk = (past_len + lane % nq) // CHUNK
    n_adm = jnp.minimum((q_chunk + 1) * CHUNK, n_keys)
    nk = sc_scr.shape[0] // tk

    def score_tile(kt, carry):
        off = pl.multiple_of(kt * tk, tk)
        z = _dot(kidx_ref[0, pl.ds(off, tk), :], qi_ref[0])
        part = w_ref[0] * jnp.maximum(z, 0.0)
        shift = lanes // 2
        while shift >= nq:
            part = part + pltpu.roll(part, shift, axis=1)
            shift //= 2
        s_idx = _key_index(kt, tk)
        admissible = (s_idx // CHUNK <= q_chunk) & (s_idx < n_keys)
        return _store_scores(sc_scr, off, tk, part, admissible, carry)

    s_lo, s_hi = lax.fori_loop(0, nk, score_tile, _score_range_init(lanes))
    s_lo = jnp.min(s_lo, axis=0, keepdims=True)
    s_hi = jnp.max(s_hi, axis=0, keepdims=True)
    thr = _select_threshold(sc_scr, nk, tk, n_adm, s_lo, s_hi, topk)

    ones_rows = jnp.ones((ACC_ROWS - DH_A, tk), MXU_DTYPE)

    def attend_tile(kt, carry):
        m_old, acc = carry
        off = pl.multiple_of(kt * tk, tk)
        keep = sc_scr[pl.ds(off, tk), :] >= thr
        s = jnp.where(keep, _dot(k_ref[0, pl.ds(off, tk), :], qa_ref[0]), -jnp.inf)
        m_new = jnp.maximum(m_old, jnp.max(s, axis=0, keepdims=True))
        m_safe = jnp.where(m_new == -jnp.inf, 0.0, m_new)
        p = _mx(jnp.exp(s - m_safe))
        vt_ext = jnp.concatenate([vt_ref[0, kt], ones_rows], axis=0)
        return m_new, jnp.exp(m_old - m_safe) * acc + _dot(vt_ext, p)

    init = (jnp.full((1, lanes), -jnp.inf, jnp.float32),
            jnp.zeros((KV_A * DH_A + ACC_ROWS - DH_A, lanes), jnp.float32))
    _, acc = lax.fori_loop(0, nk, attend_tile, init)
    group = lane // (GROUP_A * nq)
    out = acc[0:DH_A]
    for g in range(1, KV_A):
        out = jnp.where(group == g, acc[g * DH_A:(g + 1) * DH_A], out)
    o_ref[0] = (out / acc[KV_A * DH_A:KV_A * DH_A + 1]).astype(o_ref.dtype)


def _dsa_packed_attention(qi_p, qa_p, w_p, k_all, vt_all, kidx_all, *, nq, past_len, n_keys):
    b = qi_p.shape[0]
    lp = k_all.shape[1]
    tk = vt_all.shape[3]
    topk = min(TOPK_MAX, n_keys // 4)
    kern = functools.partial(_dsa_packed_kernel, tk=tk, nq=nq, past_len=past_len, n_keys=n_keys, topk=topk)
    per_batch = lambda *dims: pl.BlockSpec((1,) + dims, lambda i: (i,) + (0,) * len(dims))
    return pl.pallas_call(
        kern,
        grid=(b,),
        in_specs=[per_batch(D_IDX, LANES), per_batch(KV_A * DH_A, LANES), per_batch(1, LANES),
                  per_batch(lp, KV_A * DH_A), per_batch(lp // tk, KV_A * DH_A, tk), per_batch(lp, D_IDX)],
        out_specs=per_batch(DH_A, LANES),
        out_shape=jax.ShapeDtypeStruct((b, DH_A, LANES), MXU_DTYPE),
        scratch_shapes=[pltpu.VMEM((lp, LANES), jnp.float32)],
        compiler_params=_params(("parallel",)),
        name="dsa_packed_attention",
    )(qi_p, qa_p, w_p, k_all, vt_all, kidx_all)


def _softplus(x):
    return jnp.maximum(x, 0.0) + jnp.log(1.0 + jnp.exp(-jnp.abs(x)))


def _unit_lower_inverses(a_strict, c):
    eye = (lax.broadcasted_iota(jnp.int32, (c, c), 0) == lax.broadcasted_iota(jnp.int32, (c, c), 1)
           ).astype(jnp.float32)
    power = [-a for a in a_strict]
    inv = [eye + p for p in power]
    span = 2
    while span < c:
        pb = [_mx(p) for p in power]
        power = [_dot(p, p) for p in pb]
        inv = [i + _dot(_mx(i), _mx(p)) for i, p in zip(inv, power)]
        span *= 2
    return inv


def _gdn_kernel(qkvb_ref, gateb_ref, misc_ref, a_t_ref, conv0_ref, ssm0_ref, convw_ref,
                alog_r_ref, dtb_r_ref, alog_c_ref, dtb_c_ref, normw_ref,
                y_ref, ssm_ref, tail_scr, *, c, n_chunks):
    rows = c * n_chunks

    @pl.when(pl.program_id(1) == 0)
    def _():
        tail_scr[...] = conv0_ref[0]
        ssm_ref[...] = ssm0_ref[...]

    x = qkvb_ref[0]
    ext = jnp.concatenate([tail_scr[...], x], axis=0)
    conv = None
    for j in range(CONV_W):
        shifted = ext if j == CONV_W - 1 else pltpu.roll(ext, CONV_W - 1 - j, axis=0)
        term = shifted[SUBLANES:SUBLANES + rows] * convw_ref[j:j + 1, :]
        conv = term if conv is None else conv + term
    tail_scr[...] = x[rows - SUBLANES:rows]
    act = conv * _sigmoid(conv)

    misc = misc_ref[0]
    beta_all = _sigmoid(misc)
    g_col = -jnp.exp(alog_r_ref[...]) * _softplus(misc + dtb_r_ref[...])
    ri = lax.broadcasted_iota(jnp.int32, (rows, rows), 0)
    ci = lax.broadcasted_iota(jnp.int32, (rows, rows), 1)
    same_chunk_causal = ((ci <= ri) & (ci // c == ri // c)).astype(jnp.float32)
    big_g_col = _exact_dot(same_chunk_causal, g_col)
    upper = (lax.broadcasted_iota(jnp.int32, (LANES, LANES), 0)
             <= lax.broadcasted_iota(jnp.int32, (LANES, LANES), 1)).astype(jnp.float32)
    rc = lax.broadcasted_iota(jnp.int32, (c, c), 0)
    cc = lax.broadcasted_iota(jnp.int32, (c, c), 1)
    causal = cc <= rc
    strict = cc < rc

    big_g_row = [_exact_dot(-jnp.exp(alog_c_ref[...]) * _softplus(a_t_ref[0, n] + dtb_c_ref[...]), upper)
                 for n in range(n_chunks)]
    pairs = [(n, h) for n in range(n_chunks) for h in range(H_B)]
    tile = lambda a, n, col, width: a[n * c:(n + 1) * c, col:col + width]
    l2n = lambda a: a * lax.rsqrt(jnp.sum(a * a, axis=1, keepdims=True) + NORM_EPS)
    q = [l2n(tile(act, n, h * DK_B, DK_B)) * (DK_B ** -0.5) for n, h in pairs]
    k = [l2n(tile(act, n, (H_B + h) * DK_B, DK_B)) for n, h in pairs]
    v = [tile(act, n, 2 * H_B * DK_B + h * DV_B, DV_B) for n, h in pairs]
    gc = [tile(big_g_col, n, MISC_A + h, 1) for n, h in pairs]
    gr = [big_g_row[n][h:h + 1, :c] for n, h in pairs]
    beta = [tile(beta_all, n, MISC_B + h, 1) for n, h in pairs]
    g_last = [g[c - 1:c, :] for g in gc]
    exp_g = [jnp.exp(g) for g in gc]
    decay = [jnp.where(causal, jnp.exp(jnp.where(causal, a - b, 0.0)), 0.0) for a, b in zip(gc, gr)]
    kb = [_mx(a) for a in k]
    kk = [_nt_dot(a, a) for a in kb]
    qk = [_nt_dot(_mx(a), b) for a, b in zip(q, kb)]
    a_mat = [jnp.where(strict, b * m * d, 0.0) for b, m, d in zip(beta, kk, decay)]
    inv = _unit_lower_inverses(a_mat, c)
    rhs = [_mx(jnp.concatenate([vv * b, kx * (b * e)], axis=1)) for vv, kx, b, e in zip(v, k, beta, exp_g)]
    sol = [_dot(_mx(i), r) for i, r in zip(inv, rhs)]
    u = [s[:, :DV_B] for s in sol]
    w_qd = [_mx(jnp.concatenate([s[:, DV_B:], a * e], axis=0)) for s, a, e in zip(sol, q, exp_g)]
    qk = [_mx(m * d) for m, d in zip(qk, decay)]
    k_dec = [_mx(a * jnp.exp(gl - g)) for a, gl, g in zip(k, g_last, gc)]
    g_tot = [jnp.exp(gl) for gl in g_last]

    for n in range(n_chunks):
        ids = [n * H_B + h for h in range(H_B)]
        state = [ssm_ref[0, h] for h in range(H_B)]
        both = [_dot(w_qd[i], _mx(s)) for i, s in zip(ids, state)]
        v_new = [_mx(u[i] - b[:c]) for i, b in zip(ids, both)]
        delta = [_tn_dot(k_dec[i], vn) for i, vn in zip(ids, v_new)]
        for h, i in enumerate(ids):
            ssm_ref[0, h] = state[h] * g_tot[i] + delta[h]
        o = [b[c:] + _dot(qk[i], vn) for i, b, vn in zip(ids, both, v_new)]
        for h in range(H_B):
            oh = o[h] * lax.rsqrt(jnp.mean(o[h] * o[h], axis=1, keepdims=True) + NORM_EPS) * normw_ref[...]
            gate = gateb_ref[0, n * c:(n + 1) * c, h * DV_B:(h + 1) * DV_B]
            y_ref[0, n * c:(n + 1) * c, h * DV_B:(h + 1) * DV_B] = (oh * (gate * _sigmoid(gate))).astype(y_ref.dtype)


def _gated_delta(qkvb, gateb, misc, a_t, conv0, ssm0, conv_w, a_log, dt_bias, norm_w):
    b, t, _ = qkvb.shape
    c = min(CHUNK, t)
    n_chunks = max(1, min(4, t // c))
    rows = c * n_chunks
    in_slab = lambda v: jnp.pad(v, (MISC_A, MISC_W_COLS - MISC_A - H_B)).reshape(1, MISC_W_COLS)
    in_rows = lambda v: jnp.pad(v, (0, SUBLANES - H_B)).reshape(SUBLANES, 1)
    small = [in_slab(a_log), in_slab(dt_bias), in_rows(a_log), in_rows(dt_bias), norm_w.reshape(1, DV_B)]
    return pl.pallas_call(
        functools.partial(_gdn_kernel, c=c, n_chunks=n_chunks),
        grid=(b, t // rows),
        in_specs=[
            pl.BlockSpec((1, rows, CONV_DIM), lambda i, j: (i, j, 0)),
            pl.BlockSpec((1, rows, H_B * DV_B), lambda i, j: (i, j, 0)),
            pl.BlockSpec((1, rows, MISC_W_COLS), lambda i, j: (i, j, 0)),
            pl.BlockSpec((1, n_chunks, SUBLANES, LANES), lambda i, j: (i, j, 0, 0)),
            pl.BlockSpec((1, SUBLANES, CONV_DIM), lambda i, j: (i, 0, 0)),
            pl.BlockSpec((1, H_B, DK_B, DV_B), lambda i, j: (i, 0, 0, 0)),
            _const_spec(conv_w.shape),
        ] + [_const_spec(s.shape) for s in small],
        out_specs=[
            pl.BlockSpec((1, rows, H_B * DV_B), lambda i, j: (i, j, 0)),
            pl.BlockSpec((1, H_B, DK_B, DV_B), lambda i, j: (i, 0, 0, 0)),
        ],
        out_shape=[jax.ShapeDtypeStruct((b, t, H_B * DV_B), MXU_DTYPE),
                   jax.ShapeDtypeStruct((b, H_B, DK_B, DV_B), jnp.float32)],
        scratch_shapes=[pltpu.VMEM((SUBLANES, CONV_DIM), jnp.float32)],
        compiler_params=_params(("parallel", "arbitrary")),
        name="gated_delta",
    )(qkvb, gateb, misc, a_t, conv0, ssm0, conv_w, *small)


def _layer_norm(x, g, b):
    mu = jnp.mean(x, axis=-1, keepdims=True)
    xc = x - mu
    var = jnp.mean(xc * xc, axis=-1, keepdims=True)
    return xc * lax.rsqrt(var + LN_EPS) * g + b


def _merge_kernel(x_ref, gates_ref, ya_ref, yb_ref, bgate_ref, wpa_ref, wpb_ref, wout_ref, g_ref, b_ref, o_ref):
    gates = _sigmoid(gates_ref[...] + bgate_ref[...])
    mixed = (gates[:, :D_MODEL] * _dot(ya_ref[...], wpa_ref[...])
             + gates[:, D_MODEL:] * _dot(yb_ref[...], wpb_ref[...]))
    mix_out = _dot(_mx(mixed), wout_ref[...])
    o_ref[...] = _layer_norm(DEEPNORM_ALPHA * x_ref[...] + mix_out, g_ref[...], b_ref[...])


def _merge(x2d, gates, ya, yb, b_gate, wpa, wpb, wout, ln_g, ln_b):
    n = x2d.shape[0]
    tm = min(FFN_TILE, n)
    row = lambda w: pl.BlockSpec((tm, w), lambda i: (i, 0))
    consts = [b_gate.reshape(1, -1), wpa, wpb, wout, ln_g.reshape(1, -1), ln_b.reshape(1, -1)]
    return pl.pallas_call(
        _merge_kernel,
        grid=(n // tm,),
        in_specs=[row(D_MODEL), row(2 * D_MODEL), row(H_A * DH_A), row(H_B * DV_B)]
        + [_const_spec(c.shape) for c in consts],
        out_specs=row(D_MODEL),
        out_shape=jax.ShapeDtypeStruct((n, D_MODEL), jnp.float32),
        compiler_params=_params(("parallel",)),
        name="merge_out_ln",
    )(x2d, gates, ya, yb, *consts)


def _ffn_kernel(x_ref, wup_ref, wdown_ref, g_ref, b_ref, o_ref):
    x = x_ref[...]
    xb = _mx(x)
    acc = jnp.zeros(x.shape, jnp.float32)
    for c0 in range(0, D_FF, F_CHUNK):
        f_gate = _dot(xb, wup_ref[:, c0:c0 + F_CHUNK])
        f_up = _dot(xb, wup_ref[:, D_FF + c0:D_FF + c0 + F_CHUNK])
        hidden = f_gate * _sigmoid(f_gate) * f_up
        acc = acc + _dot(_mx(hidden), wdown_ref[c0:c0 + F_CHUNK, :])
    o_ref[...] = _layer_norm(DEEPNORM_ALPHA * x + acc, g_ref[...], b_ref[...])


def _ffn(x2d, w_up, w_down, ln_g, ln_b):
    n = x2d.shape[0]
    tm = min(FFN_TILE, n)
    row = pl.BlockSpec((tm, D_MODEL), lambda i: (i, 0))
    consts = [w_up, w_down, ln_g.reshape(1, -1), ln_b.reshape(1, -1)]
    return pl.pallas_call(
        _ffn_kernel,
        grid=(n // tm,),
        in_specs=[row] + [_const_spec(c.shape) for c in consts],
        out_specs=row,
        out_shape=jax.ShapeDtypeStruct((n, D_MODEL), jnp.float32),
        compiler_params=_params(("parallel",)),
        name="swiglu_ln",
    )(x2d, *consts)


def _round_up(a, m):
    return (a + m - 1) // m * m


def _trunk_layer(x, past_k, past_v, past_kidx, conv_state, ssm_state, prm):
    (w_rows, w_cols, b_gate, conv_w, a_log, dt_bias, norm_w, wpa, wpb, wout, ln1_g, ln1_b, w_up, w_down, ln2_g,
     ln2_b) = prm
    b, t, _ = x.shape
    n = b * t
    assert n % TOKEN_TILE == 0
    past_len = past_k.shape[1]
    n_keys = past_len + t
    x2d = x.reshape(n, D_MODEL)
    ka, va, misc, qkvb, gateb, gates, kb, kidxb, qat, qit, vt, wt = _in_proj(x2d, w_rows, w_cols)

    k_new = ka.reshape(b, t, KV_A, DH_A)
    v_new = va.reshape(b, t, KV_A, DH_A)
    kidx_new = misc[:, MISC_KIDX:MISC_KIDX + D_IDX].reshape(b, t, D_IDX)
    if past_len == 0 and t % TOKEN_TILE == 0:
        k_all = kb.reshape(b, t, KV_A * DH_A)
        kidx_all = kidxb.reshape(b, t, D_IDX)
        vt_all = vt.reshape(b, t // KEY_TILE, KV_A * DH_A, KEY_TILE)
        y_a = _dsa_attention(qat, qit, wt, k_all, vt_all, kidx_all, batch=b, past_len=past_len, n_keys=n_keys,
                             n_query=t)
        y_a = y_a.reshape(n, H_A * DH_A)
    else:
        assert H_A * t == LANES and H_I == H_A and TOKEN_TILE % t == 0
        per_tile = TOKEN_TILE // t
        lp = _round_up(n_keys, PACKED_KEY_TILE)
        pad_keys = lambda a: jnp.pad(a, ((0, 0), (0, lp - n_keys), (0, 0)))
        k_all = pad_keys(jnp.concatenate([_mx(past_k).reshape(b, past_len, -1), kb.reshape(b, t, -1)], axis=1))
        kidx_all = pad_keys(jnp.concatenate([_mx(past_kidx), kidxb.reshape(b, t, -1)], axis=1))
        v_all = pad_keys(jnp.concatenate([_mx(past_v).reshape(b, past_len, -1), _mx(va).reshape(b, t, -1)], axis=1))
        vt_all = jnp.transpose(v_all.reshape(b, lp // PACKED_KEY_TILE, PACKED_KEY_TILE, KV_A * DH_A), (0, 1, 3, 2))

        def heads_on_lanes(a, d):
            a = a.reshape(-1, H_A, d, per_tile, t)
            return jnp.transpose(a, (0, 3, 2, 1, 4)).reshape(b, d, H_A, t)

        qi_p = heads_on_lanes(qit, D_IDX).reshape(b, D_IDX, LANES)
        qa_hd = heads_on_lanes(qat, DH_A)
        head_group = (jnp.arange(H_A) // GROUP_A)[None, None, :, None]
        qa_p = jnp.concatenate([jnp.where(head_group == g, qa_hd, 0) for g in range(KV_A)], axis=1)
        w_p = heads_on_lanes(wt, 1).reshape(b, 1, LANES)
        y_t = _dsa_packed_attention(qi_p, qa_p.reshape(b, KV_A * DH_A, LANES), w_p, k_all, vt_all, kidx_all,
                                    nq=t, past_len=past_len, n_keys=n_keys)
        y_a = jnp.transpose(y_t.reshape(b, DH_A, H_A, t), (0, 3, 2, 1)).reshape(n, H_A * DH_A)

    c = min(CHUNK, t)
    a_t = misc[:, MISC_A:MISC_A + SUBLANES].reshape(b, t // c, c, SUBLANES)
    a_t = jnp.pad(jnp.transpose(a_t, (0, 1, 3, 2)), ((0, 0), (0, 0), (0, 0), (0, LANES - c)))
    qkvb3 = qkvb.reshape(b, t, CONV_DIM)
    conv0 = jnp.pad(conv_state, ((0, 0), (SUBLANES - (CONV_W - 1), 0), (0, 0)))
    y_b, new_ssm = _gated_delta(qkvb3, gateb.reshape(b, t, -1), misc.reshape(b, t, -1), a_t, conv0,
                                ssm_state, conv_w, a_log, dt_bias, norm_w)
    new_conv = jnp.concatenate([conv_state, qkvb3], axis=1)[:, t:]

    x1 = _merge(x2d, gates, y_a, y_b.reshape(n, -1), b_gate, wpa, wpb, wout, ln1_g, ln1_b)
    x2 = _ffn(x1, w_up, w_down, ln2_g, ln2_b)
    return x2.reshape(b, t, D_MODEL), k_new, v_new, kidx_new, new_conv, new_ssm


def _layer_params(l, w_in, b_gate, conv_w, a_log, dt_bias, gdn_norm_w, w_proj_a, w_proj_b, w_out, ln1_g, ln1_b,
                  w_up, w_down, ln2_g, ln2_b):
    w_rows, w_cols = _arrange_w_in(w_in[l])
    return (_mx(w_rows), _mx(w_cols), b_gate[l], conv_w[l], a_log[l], dt_bias[l], gdn_norm_w[l],
            _mx(w_proj_a[l]), _mx(w_proj_b[l]), _mx(w_out[l]), ln1_g[l], ln1_b[l],
            _mx(w_up[l]), _mx(w_down[l]), ln2_g[l], ln2_b[l])


def kernel(x_prompt, x_sample, cache_k, cache_v, cache_kidx, state_conv, state_ssm, w_in, b_gate, conv_w, a_log,
           dt_bias, gdn_norm_w, w_proj_a, w_proj_b, w_out, ln1_g, ln1_b, w_up, w_down, ln2_g, ln2_b):
    yp, ys = x_prompt, x_sample
    bp = x_prompt.shape[0]
    dt = x_prompt.dtype
    st_p, st_s = [], []
    for l in range(DEPTH):
        prm = _layer_params(l, w_in, b_gate, conv_w, a_log, dt_bias, gdn_norm_w, w_proj_a, w_proj_b, w_out,
                            ln1_g, ln1_b, w_up, w_down, ln2_g, ln2_b)
        empty_kv = jnp.zeros((bp, 0, KV_A, DH_A), dt)
        yp, *sp = _trunk_layer(yp, empty_kv, empty_kv, jnp.zeros((bp, 0, D_IDX), dt),
                               jnp.zeros((bp, CONV_W - 1, CONV_DIM), dt),
                               jnp.zeros((bp, H_B, DK_B, DV_B), dt), prm)
        ys, *ss = _trunk_layer(ys, cache_k[l], cache_v[l], cache_kidx[l], state_conv[l], state_ssm[l], prm)
        st_p.append(sp)
        st_s.append(ss)
    stack = lambda states, i: jnp.stack([s[i] for s in states], axis=0)
    return (yp, ys) + tuple(stack(st_p, i) for i in range(5)) + tuple(stack(st_s, i) for i in range(5))
```

```python
import functools
import math

import numpy as np
import jax
import jax.numpy as jnp
from jax import lax
from jax.experimental import pallas as pl
from jax.experimental.pallas import tpu as pltpu

D_MODEL = 1024
DEPTH = 2
CHUNK = 64
H_A = 8
KV_A = 2
DH_A = 64
GROUP_A = H_A // KV_A
H_I = 8
D_IDX = 64
TOPK_MAX = 256
INDEX_SCALE = (H_I * D_IDX) ** -0.5
H_B = 4
DK_B = 128
DV_B = 128
CONV_W = 4
CONV_DIM = 2 * H_B * DK_B + H_B * DV_B
D_FF = int(math.ceil(8 * D_MODEL / 3 / 256)) * 256
DEEPNORM_ALPHA = (2 * DEPTH) ** 0.25
LN_EPS = 1e-5
NORM_EPS = 1e-6
IN_SIZES = (H_A * DH_A, KV_A * DH_A, KV_A * DH_A, H_I * D_IDX, D_IDX, H_I, CONV_DIM, H_B, H_B,
            H_B * DV_B, 2 * D_MODEL)

MXU_DTYPE = jnp.bfloat16
VMEM_LIMIT_BYTES = 56 * 1024 * 1024
LANES = 128
SUBLANES = 8
FLT_MAX = float(np.finfo(np.float32).max)

TOKEN_TILE = 256
FFN_TILE = 512
KEY_TILE = 256
PACKED_KEY_TILE = 512
ATTN_GROUP = 2
F_CHUNK = 256

MISC_KIDX = 0
MISC_A = D_IDX + H_I
MISC_B = MISC_A + H_B
MISC_W_COLS = LANES

T_QA = 0
T_QI = H_A * DH_A
T_V = T_QI + H_I * D_IDX
T_W = T_V + KV_A * DH_A
T_ROWS = T_W + 16
ACC_ROWS = DH_A + 16


def _dot(a, b):
    return jnp.dot(a, b, preferred_element_type=jnp.float32)


def _nt_dot(a, b):
    return lax.dot_general(a, b, (((1,), (1,)), ((), ())), preferred_element_type=jnp.float32)


def _tn_dot(a, b):
    return lax.dot_general(a, b, (((0,), (0,)), ((), ())), preferred_element_type=jnp.float32)


def _exact_dot(a, b):
    return jnp.dot(a, b, preferred_element_type=jnp.float32, precision=lax.Precision.HIGHEST)


def _mx(a):
    return a.astype(MXU_DTYPE)


def _sigmoid(x):
    return 1.0 / (1.0 + jnp.exp(-x))


def _params(sem):
    return pltpu.CompilerParams(dimension_semantics=sem, vmem_limit_bytes=VMEM_LIMIT_BYTES)


def _const_spec(shape):
    zeros = (0,) * len(shape)
    return pl.BlockSpec(shape, lambda *_: zeros)


_ROW_GROUPS = (
    ("ka", KV_A * DH_A, jnp.float32),
    ("va", KV_A * DH_A, jnp.float32),
    ("misc", MISC_W_COLS, jnp.float32),
    ("qkvb", CONV_DIM, jnp.float32),
    ("gateb", H_B * DV_B, jnp.float32),
    ("gates", 2 * D_MODEL, jnp.float32),
)


def _arrange_w_in(w_in):
    offs = np.cumsum((0,) + IN_SIZES)
    col = lambda i: w_in[:, offs[i]:offs[i + 1]]
    misc = jnp.concatenate([col(4), col(5), col(7), col(8)], axis=1)
    misc = jnp.pad(misc, ((0, 0), (0, MISC_W_COLS - misc.shape[1])))
    w_rows = jnp.concatenate([col(1), col(2), misc, col(6), col(9), col(10)], axis=1)
    w_cols = jnp.concatenate([col(0), col(3), col(2), col(5)], axis=1).T
    w_cols = jnp.pad(w_cols, ((0, T_ROWS - w_cols.shape[0]), (0, 0)))
    return w_rows, w_cols


def _in_proj_kernel(x_ref, w_ref, wt_ref, ka_ref, va_ref, misc_ref, qkvb_ref, gateb_ref, gates_ref,
                    kb_ref, kidxb_ref, qat_ref, qit_ref, vt_ref, wt_out_ref):
    xb = _mx(x_ref[...])
    off = 0
    for (_, width, _), o_ref in zip(_ROW_GROUPS, (ka_ref, va_ref, misc_ref, qkvb_ref, gateb_ref, gates_ref)):
        for c0 in range(0, width, 512):
            c1 = min(c0 + 512, width)
            o_ref[:, c0:c1] = _dot(xb, w_ref[:, off + c0:off + c1])
        off += width
    kb_ref[...] = _mx(ka_ref[...])
    kidxb_ref[...] = _mx(misc_ref[:, MISC_KIDX:MISC_KIDX + D_IDX])
    qat_ref[0] = _mx(_nt_dot(wt_ref[T_QA:T_QI, :], xb) * (DH_A ** -0.5))
    qit_ref[0] = _mx(_nt_dot(wt_ref[T_QI:T_V, :], xb))
    vt_ref[0] = _mx(_nt_dot(wt_ref[T_V:T_W, :], xb))
    wt_out_ref[0] = _nt_dot(wt_ref[T_W:T_ROWS, :], xb)[:H_I] * INDEX_SCALE


def _in_proj(x2d, w_rows, w_cols):
    n = x2d.shape[0]
    tm = TOKEN_TILE
    nb = n // tm
    row = lambda w: pl.BlockSpec((tm, w), lambda i: (i, 0))
    col = lambda r: pl.BlockSpec((1, r, tm), lambda i: (i, 0, 0))
    out_specs = [row(w) for _, w, _ in _ROW_GROUPS] + [row(KV_A * DH_A), row(D_IDX)] + [
        col(H_A * DH_A), col(H_I * D_IDX), col(KV_A * DH_A), col(H_I)]
    out_shape = [jax.ShapeDtypeStruct((n, w), dt) for _, w, dt in _ROW_GROUPS] + [
        jax.ShapeDtypeStruct((n, KV_A * DH_A), MXU_DTYPE), jax.ShapeDtypeStruct((n, D_IDX), MXU_DTYPE),
        jax.ShapeDtypeStruct((nb, H_A * DH_A, tm), MXU_DTYPE), jax.ShapeDtypeStruct((nb, H_I * D_IDX, tm), MXU_DTYPE),
        jax.ShapeDtypeStruct((nb, KV_A * DH_A, tm), MXU_DTYPE), jax.ShapeDtypeStruct((nb, H_I, tm), jnp.float32)]
    return pl.pallas_call(
        _in_proj_kernel,
        grid=(nb,),
        in_specs=[row(D_MODEL), _const_spec(w_rows.shape), _const_spec(w_cols.shape)],
        out_specs=out_specs,
        out_shape=out_shape,
        compiler_params=_params(("parallel",)),
        name="in_proj",
    )(x2d, w_rows, w_cols)


def _order_key(x):
    bits = lax.bitcast_convert_type(x, jnp.int32)
    return bits ^ ((bits >> 31) & jnp.int32(0x7FFFFFFF))


def _from_order_key(key):
    return lax.bitcast_convert_type(key ^ ((key >> 31) & jnp.int32(0x7FFFFFFF)), jnp.float32)


_MAX_SUBNORMAL_KEY = 0x007FFFFF


def _order_rank(x):
    key = _order_key(x)
    return jnp.where(key > _MAX_SUBNORMAL_KEY, key - _MAX_SUBNORMAL_KEY,
                     jnp.where(key < -_MAX_SUBNORMAL_KEY - 1, key + _MAX_SUBNORMAL_KEY + 1, 0))


def _from_order_rank(rank):
    key = jnp.where(rank > 0, rank + _MAX_SUBNORMAL_KEY, jnp.where(rank < 0, rank - _MAX_SUBNORMAL_KEY - 1, 0))
    return _from_order_key(key)


FOLD_ROWS = 4 * SUBLANES


def _fold(x, op):
    return op(x.reshape(x.shape[0] // FOLD_ROWS, FOLD_ROWS, x.shape[1]), axis=0)


def _key_index(kt, tk):
    return kt * tk + lax.broadcasted_iota(jnp.int32, (tk, 1), 0)


def _select_threshold(sc_scr, nk, tk, n_adm, s_lo, s_hi, topk):
    lanes = sc_scr.shape[1]
    kf = float(topk)

    def count(pred):
        def body(kt, acc):
            off = pl.multiple_of(kt * tk, tk)
            hit = jnp.where(pred(sc_scr[pl.ds(off, tk), :], _key_index(kt, tk)), 1.0, 0.0)
            return acc + _fold(hit, jnp.sum)
        acc = lax.fori_loop(0, nk, body, jnp.zeros((FOLD_ROWS, lanes), jnp.float32))
        return jnp.sum(acc, axis=0, keepdims=True)

    n_adm_f = n_adm.astype(jnp.float32)
    take_all = n_adm_f <= kf

    def search_step(st):
        it, lo_key, hi_key, c_lo, c_hi, done = st
        lo_f = _from_order_rank(lo_key)
        hi_f = _from_order_rank(hi_key)
        interp = lo_f + (hi_f - lo_f) * ((c_lo - kf) / (c_lo - c_hi))
        middle = 0.5 * lo_f + 0.5 * hi_f
        by_value = _order_rank(jnp.where(it % 2 == 0, interp, middle))
        by_key = (lo_key >> 1) + (hi_key >> 1) + ((lo_key | hi_key) & 1)
        cand_key = jnp.where(it < 2, it, jnp.where(it < 40, by_value, by_key))
        cand_key = jnp.minimum(jnp.maximum(cand_key, lo_key + 1), hi_key)
        cand = _from_order_rank(cand_key)
        c = count(lambda x, _: x >= cand)
        active = done < 0.5
        up = active & (c >= kf)
        down = active & (c < kf)
        lo_key = jnp.where(up, cand_key, lo_key)
        c_lo = jnp.where(up, c, c_lo)
        hi_key = jnp.where(down, cand_key - 1, hi_key)
        c_hi = jnp.where(down, c, c_hi)
        done = jnp.where((c_lo == kf) | (lo_key >= hi_key), 1.0, done)
        return it + 1, lo_key, hi_key, c_lo, c_hi, done

    def search_cond(st):
        return (st[0] < 80) & (jnp.min(st[5]) < 0.5)

    lo_key0 = _order_rank(s_lo)
    hi_key0 = _order_rank(s_hi)
    done0 = jnp.where(take_all | (lo_key0 >= hi_key0), 1.0, 0.0)
    st = (jnp.int32(0), lo_key0, hi_key0, n_adm_f, jnp.zeros((1, lanes), jnp.float32), done0)
    _, lo_key, _, c_lo, c_hi, _ = lax.while_loop(search_cond, lambda s: search_step(search_step(s)), st)
    thr = jnp.where(take_all, -FLT_MAX, _from_order_rank(lo_key))

    excess = (c_lo > kf) & jnp.logical_not(take_all)
    n_rows = sc_scr.shape[0]

    @pl.when(jnp.max(jnp.where(excess, 1.0, 0.0)) > 0.5)
    def _():
        need = kf - c_hi
        n_bits = max(1, int(n_rows - 1).bit_length())

        def idx_step(i, bound):
            cand = bound | lax.shift_left(jnp.int32(1), n_bits - 1 - i)
            below = count(lambda x, idx: (x == thr) & (idx < cand))
            return jnp.where(below < need, cand, bound)

        bound = lax.fori_loop(0, n_bits, idx_step, jnp.zeros((1, lanes), jnp.int32))
        bound = jnp.where(excess, bound, jnp.int32(n_rows))

        def drop_tile(kt, carry):
            off = pl.multiple_of(kt * tk, tk)
            x = sc_scr[pl.ds(off, tk), :]
            sc_scr[pl.ds(off, tk), :] = jnp.where((x == thr) & (_key_index(kt, tk) > bound), -jnp.inf, x)
            return carry

        lax.fori_loop(0, nk, drop_tile, 0)

    return thr


def _store_scores(sc_scr, off, tk, score, admissible, carry):
    lo, hi = carry
    if admissible is None:
        sc_scr[pl.ds(off, tk), :] = score
        return jnp.minimum(lo, _fold(score, jnp.min)), jnp.maximum(hi, _fold(score, jnp.max))
    masked = jnp.where(admissible, score, -jnp.inf)
    sc_scr[pl.ds(off, tk), :] = masked
    lo = jnp.minimum(lo, _fold(jnp.where(admissible, score, jnp.inf), jnp.min))
    hi = jnp.maximum(hi, _fold(masked, jnp.max))
    return lo, hi


def _score_range_init(lanes):
    return (jnp.full((FOLD_ROWS, lanes), jnp.inf, jnp.float32), jnp.full((FOLD_ROWS, lanes), -jnp.inf, jnp.float32))


def _dsa_kernel(qat_ref, qit_ref, wt_ref, k_ref, vt_ref, kidx_ref, o_ref, sc_scr, acc_scr, *,
                tq, tk, past_len, n_keys, n_query, topk):
    iq = pl.program_id(1)
    last_q = past_len + n_query - 1
    q_pos0 = past_len + iq * tq
    q_pos = jnp.minimum(q_pos0 + lax.broadcasted_iota(jnp.int32, (1, tq), 1), last_q)
    q_chunk = q_pos // CHUNK
    n_adm = jnp.minimum((q_chunk + 1) * CHUNK, n_keys)
    n_vis = jnp.minimum((jnp.minimum(q_pos0 + tq - 1, last_q) // CHUNK + 1) * CHUNK, n_keys)
    nk = (n_vis + tk - 1) // tk

    def score_tile(kt, carry):
        off = pl.multiple_of(kt * tk, tk)
        kidx_t = kidx_ref[0, pl.ds(off, tk), :]
        score = jnp.zeros((tk, tq), jnp.float32)
        for h in range(H_I):
            z = _dot(kidx_t, qit_ref[0, h * D_IDX:(h + 1) * D_IDX, :])
            score = score + wt_ref[0, h:h + 1, :] * jnp.maximum(z, 0.0)

        def with_mask(c):
            s_idx = _key_index(kt, tk)
            admissible = (s_idx // CHUNK <= q_chunk) & (s_idx < n_keys)
            return _store_scores(sc_scr, off, tk, score, admissible, c)

        interior = ((off + tk - 1) // CHUNK <= q_pos0 // CHUNK) & (off + tk <= n_keys)
        return lax.cond(interior, lambda c: _store_scores(sc_scr, off, tk, score, None, c), with_mask, carry)

    s_lo, s_hi = lax.fori_loop(0, nk, score_tile, _score_range_init(tq))
    s_lo = jnp.min(s_lo, axis=0, keepdims=True)
    s_hi = jnp.max(s_hi, axis=0, keepdims=True)
    thr = _select_threshold(sc_scr, nk, tk, n_adm, s_lo, s_hi, topk)

    acc_scr[...] = jnp.zeros(acc_scr.shape, jnp.float32)
    ones_rows = jnp.ones((ACC_ROWS - DH_A, tk), MXU_DTYPE)

    def masked_scores(kt):
        off = pl.multiple_of(kt * tk, tk)
        bias = jnp.where(sc_scr[pl.ds(off, tk), :] >= thr, 0.0, -jnp.inf)
        k_t = [k_ref[0, pl.ds(off, tk), g * DH_A:(g + 1) * DH_A] for g in range(KV_A)]
        return [_mx(_dot(k_t[h // GROUP_A], qat_ref[0, h * DH_A:(h + 1) * DH_A, :]) + bias) for h in range(H_A)]

    def softmax_pv(kt, s, m_all):
        m_out = []
        for g in range(KV_A):
            heads = range(g * GROUP_A, (g + 1) * GROUP_A)
            vt_ext = jnp.concatenate([vt_ref[0, kt, g * DH_A:(g + 1) * DH_A, :], ones_rows], axis=0)
            m_new = [jnp.maximum(m_all[h], jnp.max(s[h], axis=0, keepdims=True).astype(jnp.float32))
                     for h in heads]
            m_safe = [jnp.where(m == -jnp.inf, 0.0, m) for m in m_new]
            p = [jnp.exp(s[h] - _mx(m)) for h, m in zip(heads, m_safe)]
            pv = [_dot(vt_ext, p_h) for p_h in p]
            for h, m, pv_h in zip(heads, m_safe, pv):
                acc_scr[h] = jnp.exp(m_all[h] - m) * acc_scr[h] + pv_h
            m_out += m_new
        return tuple(m_out)

    def attend_group(i, m_all):
        tiles = [ATTN_GROUP * i + j for j in range(ATTN_GROUP)]
        scores = [masked_scores(kt) for kt in tiles]
        for kt, s in zip(tiles, scores):
            m_all = softmax_pv(kt, s, m_all)
        return m_all

    m_init = tuple(jnp.full((1, tq), -jnp.inf, jnp.float32) for _ in range(H_A))
    m_all = lax.fori_loop(0, nk // ATTN_GROUP, attend_group, m_init)
    lax.fori_loop(nk // ATTN_GROUP * ATTN_GROUP, nk, lambda kt, m: softmax_pv(kt, masked_scores(kt), m), m_all)

    outs = [acc_scr[h, :DH_A, :] / acc_scr[h, DH_A:DH_A + 1, :] for h in range(H_A)]
    o_ref[0] = jnp.concatenate(outs, axis=0).T.astype(o_ref.dtype)


def _dsa_attention(qat, qit, wt, k_all, vt_all, kidx_all, *, batch, past_len, n_keys, n_query):
    tq = qat.shape[2]
    nq = qat.shape[0] // batch
    lp = k_all.shape[1]
    tk = KEY_TILE
    topk = min(TOPK_MAX, n_keys // 4)
    kern = functools.partial(_dsa_kernel, tq=tq, tk=tk, past_len=past_len, n_keys=n_keys, n_query=n_query, topk=topk)
    qspec = lambda rows: pl.BlockSpec((1, rows, tq), lambda i, j: (i * nq + j, 0, 0))
    return pl.pallas_call(
        kern,
        grid=(batch, nq),
        in_specs=[
            qspec(H_A * DH_A), qspec(H_I * D_IDX), qspec(H_I),
            pl.BlockSpec((1, lp, KV_A * DH_A), lambda i, j: (i, 0, 0)),
            pl.BlockSpec((1, lp // tk, KV_A * DH_A, tk), lambda i, j: (i, 0, 0, 0)),
            pl.BlockSpec((1, lp, D_IDX), lambda i, j: (i, 0, 0)),
        ],
        out_specs=pl.BlockSpec((1, tq, H_A * DH_A), lambda i, j: (i, j, 0)),
        out_shape=jax.ShapeDtypeStruct((batch, nq * tq, H_A * DH_A), MXU_DTYPE),
        scratch_shapes=[pltpu.VMEM((lp, tq), jnp.float32), pltpu.VMEM((H_A, ACC_ROWS, tq), jnp.float32)],
        compiler_params=_params(("parallel", "arbitrary")),
        name="dsa_attention",
    )(qat, qit, wt, k_all, vt_all, kidx_all)


def _dsa_packed_kernel(qi_ref, qa_ref, w_ref, k_ref, vt_ref, kidx_ref, o_ref, sc_scr, *,
                       tk, nq, past_len, n_keys, topk):
    lanes = LANES
    lane = lax.broadcasted_iota(jnp.int32, (1, lanes), 1)
    q_chunk = (past_len + lane % nq) // CHUNK
    n_adm = jnp.minimum((q_chunk + 1) * CHUNK, n_keys)
    nk = sc_scr.shape[0] // tk

    tiles = [slice(kt * tk, (kt + 1) * tk) for kt in range(nk)]
    z_all = [_dot(kidx_ref[0, rows, :], qi_ref[0]) for rows in tiles]
    carry = _score_range_init(lanes)
    for kt, (rows, z) in enumerate(zip(tiles, z_all)):
        part = w_ref[0] * jnp.maximum(z, 0.0)
        shift = lanes // 2
        while shift >= nq:
            part = part + pltpu.roll(part, shift, axis=1)
            shift //= 2
        s_idx = _key_index(kt, tk)
        admissible = (s_idx // CHUNK <= q_chunk) & (s_idx < n_keys)
        carry = _store_scores(sc_scr, kt * tk, tk, part, admissible, carry)
    s_lo = jnp.min(carry[0], axis=0, keepdims=True)
    s_hi = jnp.max(carry[1], axis=0, keepdims=True)
    thr = _select_threshold(sc_scr, nk, tk, n_adm, s_lo, s_hi, topk)

    ones_rows = jnp.ones((ACC_ROWS - DH_A, tk), MXU_DTYPE)
    s_all = [_mx(_dot(k_ref[0, rows, :], qa_ref[0]) + jnp.where(sc_scr[rows, :] >= thr, 0.0, -jnp.inf))
             for rows in tiles]
    m_all = s_all[0].max(axis=0, keepdims=True)
    for s in s_all[1:]:
        m_all = jnp.maximum(m_all, s.max(axis=0, keepdims=True))
    m_safe = jnp.where(m_all == -jnp.inf, 0.0, m_all)
    p_all = [jnp.exp(s - m_safe) for s in s_all]
    acc = jnp.zeros((KV_A * DH_A + ACC_ROWS - DH_A, lanes), jnp.float32)
    for kt, p in enumerate(p_all):
        acc = acc + _dot(jnp.concatenate([vt_ref[0, kt], ones_rows], axis=0), p)
    group = lane // (GROUP_A * nq)
    out = acc[0:DH_A]
    for g in range(1, KV_A):
        out = jnp.where(group == g, acc[g * DH_A:(g + 1) * DH_A], out)
    o_ref[0] = (out / acc[KV_A * DH_A:KV_A * DH_A + 1]).astype(o_ref.dtype)


def _dsa_packed_attention(qi_p, qa_p, w_p, k_all, vt_all, kidx_all, *, nq, past_len, n_keys):
    b = qi_p.shape[0]
    lp = k_all.shape[1]
    tk = vt_all.shape[3]
    topk = min(TOPK_MAX, n_keys // 4)
    kern = functools.partial(_dsa_packed_kernel, tk=tk, nq=nq, past_len=past_len, n_keys=n_keys, topk=topk)
    per_batch = lambda *dims: pl.BlockSpec((1,) + dims, lambda i: (i,) + (0,) * len(dims))
    return pl.pallas_call(
        kern,
        grid=(b,),
        in_specs=[per_batch(D_IDX, LANES), per_batch(KV_A * DH_A, LANES), per_batch(1, LANES),
                  per_batch(lp, KV_A * DH_A), per_batch(lp // tk, KV_A * DH_A, tk), per_batch(lp, D_IDX)],
        out_specs=per_batch(DH_A, LANES),
        out_shape=jax.ShapeDtypeStruct((b, DH_A, LANES), MXU_DTYPE),
        scratch_shapes=[pltpu.VMEM((lp, LANES), jnp.float32)],
        compiler_params=_params(("parallel",)),
        name="dsa_packed_attention",
    )(qi_p, qa_p, w_p, k_all, vt_all, kidx_all)


def _softplus(x):
    return jnp.maximum(x, 0.0) + jnp.log(1.0 + jnp.exp(-jnp.abs(x)))


def _unit_lower_inverses(a_strict, c):
    eye = (lax.broadcasted_iota(jnp.int32, (c, c), 0) == lax.broadcasted_iota(jnp.int32, (c, c), 1)
           ).astype(jnp.float32)
    power = [-a for a in a_strict]
    inv = [eye + p for p in power]
    span = 2
    while span < c:
        pb = [_mx(p) for p in power]
        power = [_dot(p, p) for p in pb]
        inv = [i + _dot(_mx(i), _mx(p)) for i, p in zip(inv, power)]
        span *= 2
    return inv


def _gdn_kernel(qkvb_ref, gateb_ref, misc_ref, a_t_ref, conv0_ref, ssm0_ref, convw_ref,
                alog_r_ref, dtb_r_ref, alog_c_ref, dtb_c_ref, normw_ref,
                y_ref, ssm_ref, tail_scr, *, c, n_chunks):
    rows = c * n_chunks

    @pl.when(pl.program_id(1) == 0)
    def _():
        tail_scr[...] = conv0_ref[0]
        ssm_ref[...] = ssm0_ref[...]

    x = qkvb_ref[0]
    ext = jnp.concatenate([tail_scr[...], x], axis=0)
    conv = None
    for j in range(CONV_W):
        shifted = ext if j == CONV_W - 1 else pltpu.roll(ext, CONV_W - 1 - j, axis=0)
        term = shifted[SUBLANES:SUBLANES + rows] * convw_ref[j:j + 1, :]
        conv = term if conv is None else conv + term
    tail_scr[...] = x[rows - SUBLANES:rows]
    act = conv * _sigmoid(conv)

    misc = misc_ref[0]
    beta_all = _sigmoid(misc)
    g_col = -jnp.exp(alog_r_ref[...]) * _softplus(misc + dtb_r_ref[...])
    ri = lax.broadcasted_iota(jnp.int32, (rows, rows), 0)
    ci = lax.broadcasted_iota(jnp.int32, (rows, rows), 1)
    same_chunk_causal = ((ci <= ri) & (ci // c == ri // c)).astype(jnp.float32)
    big_g_col = _exact_dot(same_chunk_causal, g_col)
    upper = (lax.broadcasted_iota(jnp.int32, (LANES, LANES), 0)
             <= lax.broadcasted_iota(jnp.int32, (LANES, LANES), 1)).astype(jnp.float32)
    rc = lax.broadcasted_iota(jnp.int32, (c, c), 0)
    cc = lax.broadcasted_iota(jnp.int32, (c, c), 1)
    causal = cc <= rc
    strict = cc < rc

    big_g_row = [_exact_dot(-jnp.exp(alog_c_ref[...]) * _softplus(a_t_ref[0, n] + dtb_c_ref[...]), upper)
                 for n in range(n_chunks)]
    pairs = [(n, h) for n in range(n_chunks) for h in range(H_B)]
    tile = lambda a, n, col, width: a[n * c:(n + 1) * c, col:col + width]
    l2n = lambda a: a * lax.rsqrt(jnp.sum(a * a, axis=1, keepdims=True) + NORM_EPS)
    q = [l2n(tile(act, n, h * DK_B, DK_B)) * (DK_B ** -0.5) for n, h in pairs]
    k = [l2n(tile(act, n, (H_B + h) * DK_B, DK_B)) for n, h in pairs]
    v = [tile(act, n, 2 * H_B * DK_B + h * DV_B, DV_B) for n, h in pairs]
    gc = [tile(big_g_col, n, MISC_A + h, 1) for n, h in pairs]
    gr = [big_g_row[n][h:h + 1, :c] for n, h in pairs]
    beta = [tile(beta_all, n, MISC_B + h, 1) for n, h in pairs]
    g_last = [g[c - 1:c, :] for g in gc]
    exp_g = [jnp.exp(g) for g in gc]
    decay = [jnp.where(causal, jnp.exp(jnp.where(causal, a - b, 0.0)), 0.0) for a, b in zip(gc, gr)]
    kb = [_mx(a) for a in k]
    kk = [_nt_dot(a, a) for a in kb]
    qk = [_nt_dot(_mx(a), b) for a, b in zip(q, kb)]
    a_mat = [jnp.where(strict, b * m * d, 0.0) for b, m, d in zip(beta, kk, decay)]
    inv = _unit_lower_inverses(a_mat, c)
    rhs = [_mx(jnp.concatenate([vv * b, kx * (b * e)], axis=1)) for vv, kx, b, e in zip(v, k, beta, exp_g)]
    sol = [_dot(_mx(i), r) for i, r in zip(inv, rhs)]
    u = [s[:, :DV_B] for s in sol]
    w_qd = [_mx(jnp.concatenate([s[:, DV_B:], a * e], axis=0)) for s, a, e in zip(sol, q, exp_g)]
    qk = [_mx(m * d) for m, d in zip(qk, decay)]
    k_dec = [_mx(a * jnp.exp(gl - g)) for a, gl, g in zip(k, g_last, gc)]
    g_tot = [jnp.exp(gl) for gl in g_last]

    for n in range(n_chunks):
        ids = [n * H_B + h for h in range(H_B)]
        state = [ssm_ref[0, h] for h in range(H_B)]
        both = [_dot(w_qd[i], _mx(s)) for i, s in zip(ids, state)]
        v_new = [_mx(u[i] - b[:c]) for i, b in zip(ids, both)]
        delta = [_tn_dot(k_dec[i], vn) for i, vn in zip(ids, v_new)]
        for h, i in enumerate(ids):
            ssm_ref[0, h] = state[h] * g_tot[i] + delta[h]
        o = [b[c:] + _dot(qk[i], vn) for i, b, vn in zip(ids, both, v_new)]
        for h in range(H_B):
            oh = o[h] * lax.rsqrt(jnp.mean(o[h] * o[h], axis=1, keepdims=True) + NORM_EPS) * normw_ref[...]
            gate = gateb_ref[0, n * c:(n + 1) * c, h * DV_B:(h + 1) * DV_B]
            y_ref[0, n * c:(n + 1) * c, h * DV_B:(h + 1) * DV_B] = (oh * (gate * _sigmoid(gate))).astype(y_ref.dtype)


def _gated_delta(qkvb, gateb, misc, a_t, conv0, ssm0, conv_w, a_log, dt_bias, norm_w):
    b, t, _ = qkvb.shape
    c = min(CHUNK, t)
    n_chunks = max(1, min(4, t // c))
    rows = c * n_chunks
    in_slab = lambda v: jnp.pad(v, (MISC_A, MISC_W_COLS - MISC_A - H_B)).reshape(1, MISC_W_COLS)
    in_rows = lambda v: jnp.pad(v, (0, SUBLANES - H_B)).reshape(SUBLANES, 1)
    small = [in_slab(a_log), in_slab(dt_bias), in_rows(a_log), in_rows(dt_bias), norm_w.reshape(1, DV_B)]
    return pl.pallas_call(
        functools.partial(_gdn_kernel, c=c, n_chunks=n_chunks),
        grid=(b, t // rows),
        in_specs=[
            pl.BlockSpec((1, rows, CONV_DIM), lambda i, j: (i, j, 0)),
            pl.BlockSpec((1, rows, H_B * DV_B), lambda i, j: (i, j, 0)),
            pl.BlockSpec((1, rows, MISC_W_COLS), lambda i, j: (i, j, 0)),
            pl.BlockSpec((1, n_chunks, SUBLANES, LANES), lambda i, j: (i, j, 0, 0)),
            pl.BlockSpec((1, SUBLANES, CONV_DIM), lambda i, j: (i, 0, 0)),
            pl.BlockSpec((1, H_B, DK_B, DV_B), lambda i, j: (i, 0, 0, 0)),
            _const_spec(conv_w.shape),
        ] + [_const_spec(s.shape) for s in small],
        out_specs=[
            pl.BlockSpec((1, rows, H_B * DV_B), lambda i, j: (i, j, 0)),
            pl.BlockSpec((1, H_B, DK_B, DV_B), lambda i, j: (i, 0, 0, 0)),
        ],
        out_shape=[jax.ShapeDtypeStruct((b, t, H_B * DV_B), MXU_DTYPE),
                   jax.ShapeDtypeStruct((b, H_B, DK_B, DV_B), jnp.float32)],
        scratch_shapes=[pltpu.VMEM((SUBLANES, CONV_DIM), jnp.float32)],
        compiler_params=_params(("parallel", "arbitrary")),
        name="gated_delta",
    )(qkvb, gateb, misc, a_t, conv0, ssm0, conv_w, *small)


def _layer_norm(x, g, b):
    mu = jnp.mean(x, axis=-1, keepdims=True)
    xc = x - mu
    var = jnp.mean(xc * xc, axis=-1, keepdims=True)
    return xc * lax.rsqrt(var + LN_EPS) * g + b


def _merge_kernel(x_ref, gates_ref, ya_ref, yb_ref, bgate_ref, wpa_ref, wpb_ref, wout_ref, g_ref, b_ref, o_ref):
    gates = _sigmoid(gates_ref[...] + bgate_ref[...])
    mixed = (gates[:, :D_MODEL] * _dot(ya_ref[...], wpa_ref[...])
             + gates[:, D_MODEL:] * _dot(yb_ref[...], wpb_ref[...]))
    mix_out = _dot(_mx(mixed), wout_ref[...])
    o_ref[...] = _layer_norm(DEEPNORM_ALPHA * x_ref[...] + mix_out, g_ref[...], b_ref[...])


def _merge(x2d, gates, ya, yb, b_gate, wpa, wpb, wout, ln_g, ln_b):
    n = x2d.shape[0]
    tm = min(FFN_TILE, n)
    row = lambda w: pl.BlockSpec((tm, w), lambda i: (i, 0))
    consts = [b_gate.reshape(1, -1), wpa, wpb, wout, ln_g.reshape(1, -1), ln_b.reshape(1, -1)]
    return pl.pallas_call(
        _merge_kernel,
        grid=(n // tm,),
        in_specs=[row(D_MODEL), row(2 * D_MODEL), row(H_A * DH_A), row(H_B * DV_B)]
        + [_const_spec(c.shape) for c in consts],
        out_specs=row(D_MODEL),
        out_shape=jax.ShapeDtypeStruct((n, D_MODEL), jnp.float32),
        compiler_params=_params(("parallel",)),
        name="merge_out_ln",
    )(x2d, gates, ya, yb, *consts)


def _ffn_kernel(x_ref, wup_ref, wdown_ref, g_ref, b_ref, o_ref):
    x = x_ref[...]
    xb = _mx(x)
    acc = jnp.zeros(x.shape, jnp.float32)
    for c0 in range(0, D_FF, F_CHUNK):
        f_gate = _dot(xb, wup_ref[:, c0:c0 + F_CHUNK])
        f_up = _dot(xb, wup_ref[:, D_FF + c0:D_FF + c0 + F_CHUNK])
        hidden = f_gate * _sigmoid(f_gate) * f_up
        acc = acc + _dot(_mx(hidden), wdown_ref[c0:c0 + F_CHUNK, :])
    o_ref[...] = _layer_norm(DEEPNORM_ALPHA * x + acc, g_ref[...], b_ref[...])


def _ffn(x2d, w_up, w_down, ln_g, ln_b):
    n = x2d.shape[0]
    tm = min(FFN_TILE, n)
    row = pl.BlockSpec((tm, D_MODEL), lambda i: (i, 0))
    consts = [w_up, w_down, ln_g.reshape(1, -1), ln_b.reshape(1, -1)]
    return pl.pallas_call(
        _ffn_kernel,
        grid=(n // tm,),
        in_specs=[row] + [_const_spec(c.shape) for c in consts],
        out_specs=row,
        out_shape=jax.ShapeDtypeStruct((n, D_MODEL), jnp.float32),
        compiler_params=_params(("parallel",)),
        name="swiglu_ln",
    )(x2d, *consts)


def _round_up(a, m):
    return (a + m - 1) // m * m


def _trunk_layer(x, past_k, past_v, past_kidx, conv_state, ssm_state, prm):
    (w_rows, w_cols, b_gate, conv_w, a_log, dt_bias, norm_w, wpa, wpb, wout, ln1_g, ln1_b, w_up, w_down, ln2_g,
     ln2_b) = prm
    b, t, _ = x.shape
    n = b * t
    assert n % TOKEN_TILE == 0
    past_len = past_k.shape[1]
    n_keys = past_len + t
    x2d = x.reshape(n, D_MODEL)
    ka, va, misc, qkvb, gateb, gates, kb, kidxb, qat, qit, vt, wt = _in_proj(x2d, w_rows, w_cols)

    k_new = ka.reshape(b, t, KV_A, DH_A)
    v_new = va.reshape(b, t, KV_A, DH_A)
    kidx_new = misc[:, MISC_KIDX:MISC_KIDX + D_IDX].reshape(b, t, D_IDX)
    if past_len == 0 and t % TOKEN_TILE == 0:
        k_all = kb.reshape(b, t, KV_A * DH_A)
        kidx_all = kidxb.reshape(b, t, D_IDX)
        vt_all = vt.reshape(b, t // KEY_TILE, KV_A * DH_A, KEY_TILE)
        y_a = _dsa_attention(qat, qit, wt, k_all, vt_all, kidx_all, batch=b, past_len=past_len, n_keys=n_keys,
                             n_query=t)
        y_a = y_a.reshape(n, H_A * DH_A)
    else:
        assert H_A * t == LANES and H_I == H_A and TOKEN_TILE % t == 0
        per_tile = TOKEN_TILE // t
        lp = _round_up(n_keys, PACKED_KEY_TILE)
        pad_keys = lambda a: jnp.pad(a, ((0, 0), (0, lp - n_keys), (0, 0)))
        k_all = pad_keys(jnp.concatenate([_mx(past_k).reshape(b, past_len, -1), kb.reshape(b, t, -1)], axis=1))
        kidx_all = pad_keys(jnp.concatenate([_mx(past_kidx), kidxb.reshape(b, t, -1)], axis=1))
        v_all = pad_keys(jnp.concatenate([_mx(past_v).reshape(b, past_len, -1), _mx(va).reshape(b, t, -1)], axis=1))
        vt_all = jnp.transpose(v_all.reshape(b, lp // PACKED_KEY_TILE, PACKED_KEY_TILE, KV_A * DH_A), (0, 1, 3, 2))

        def heads_on_lanes(a, d):
            a = a.reshape(-1, H_A, d, per_tile, t)
            return jnp.transpose(a, (0, 3, 2, 1, 4)).reshape(b, d, H_A, t)

        qi_p = heads_on_lanes(qit, D_IDX).reshape(b, D_IDX, LANES)
        qa_hd = heads_on_lanes(qat, DH_A)
        head_group = (jnp.arange(H_A) // GROUP_A)[None, None, :, None]
        qa_p = jnp.concatenate([jnp.where(head_group == g, qa_hd, 0) for g in range(KV_A)], axis=1)
        w_p = heads_on_lanes(wt, 1).reshape(b, 1, LANES)
        y_t = _dsa_packed_attention(qi_p, qa_p.reshape(b, KV_A * DH_A, LANES), w_p, k_all, vt_all, kidx_all,
                                    nq=t, past_len=past_len, n_keys=n_keys)
        y_a = jnp.transpose(y_t.reshape(b, DH_A, H_A, t), (0, 3, 2, 1)).reshape(n, H_A * DH_A)

    c = min(CHUNK, t)
    a_t = misc[:, MISC_A:MISC_A + SUBLANES].reshape(b, t // c, c, SUBLANES)
    a_t = jnp.pad(jnp.transpose(a_t, (0, 1, 3, 2)), ((0, 0), (0, 0), (0, 0), (0, LANES - c)))
    qkvb3 = qkvb.reshape(b, t, CONV_DIM)
    conv0 = jnp.pad(conv_state, ((0, 0), (SUBLANES - (CONV_W - 1), 0), (0, 0)))
    y_b, new_ssm = _gated_delta(qkvb3, gateb.reshape(b, t, -1), misc.reshape(b, t, -1), a_t, conv0,
                                ssm_state, conv_w, a_log, dt_bias, norm_w)
    new_conv = jnp.concatenate([conv_state, qkvb3], axis=1)[:, t:]

    x1 = _merge(x2d, gates, y_a, y_b.reshape(n, -1), b_gate, wpa, wpb, wout, ln1_g, ln1_b)
    x2 = _ffn(x1, w_up, w_down, ln2_g, ln2_b)
    return x2.reshape(b, t, D_MODEL), k_new, v_new, kidx_new, new_conv, new_ssm


def _layer_params(l, w_in, b_gate, conv_w, a_log, dt_bias, gdn_norm_w, w_proj_a, w_proj_b, w_out, ln1_g, ln1_b,
                  w_up, w_down, ln2_g, ln2_b):
    w_rows, w_cols = _arrange_w_in(w_in[l])
    return (_mx(w_rows), _mx(w_cols), b_gate[l], conv_w[l], a_log[l], dt_bias[l], gdn_norm_w[l],
            _mx(w_proj_a[l]), _mx(w_proj_b[l]), _mx(w_out[l]), ln1_g[l], ln1_b[l],
            _mx(w_up[l]), _mx(w_down[l]), ln2_g[l], ln2_b[l])


def kernel(x_prompt, x_sample, cache_k, cache_v, cache_kidx, state_conv, state_ssm, w_in, b_gate, conv_w, a_log,
           dt_bias, gdn_norm_w, w_proj_a, w_proj_b, w_out, ln1_g, ln1_b, w_up, w_down, ln2_g, ln2_b):
    yp, ys = x_prompt, x_sample
    bp = x_prompt.shape[0]
    dt = x_prompt.dtype
    st_p, st_s = [], []
    for l in range(DEPTH):
        prm = _layer_params(l, w_in, b_gate, conv_w, a_log, dt_bias, gdn_norm_w, w_proj_a, w_proj_b, w_out,
                            ln1_g, ln1_b, w_up, w_down, ln2_g, ln2_b)
        empty_kv = jnp.zeros((bp, 0, KV_A, DH_A), dt)
        yp, *sp = _trunk_layer(yp, empty_kv, empty_kv, jnp.zeros((bp, 0, D_IDX), dt),
                               jnp.zeros((bp, CONV_W - 1, CONV_DIM), dt),
                               jnp.zeros((bp, H_B, DK_B, DV_B), dt), prm)
        ys, *ss = _trunk_layer(ys, cache_k[l], cache_v[l], cache_kidx[l], state_conv[l], state_ssm[l], prm)
        st_p.append(sp)
        st_s.append(ss)
    stack = lambda states, i: jnp.stack([s[i] for s in states], axis=0)
    return (yp, ys) + tuple(stack(st_p, i) for i in range(5)) + tuple(stack(st_s, i) for i in range(5))
```
